```python
import math
import jax, jax.numpy as jnp
from jax import lax
import numpy as np

D_MODEL = 1024
BATCH = 2
SEQ = 16384
DEPTH = 4
DEC_BATCH = 8
DEC_SEQ = 4096
PAST_LEN = 128

N_HEADS = 8
QK_NOPE_DIM = 64
QK_ROPE_DIM = 32
QK_DIM = QK_NOPE_DIM + QK_ROPE_DIM
V_HEAD_DIM = 64
Q_LORA_RANK = 256
KV_LORA_RANK = 128
ATTN_WIDTH = N_HEADS * V_HEAD_DIM
ROPE_THETA = 10000.0
Q_BLOCK = 128
HYENA_WIDTH = D_MODEL - ATTN_WIDTH
HYENA_ORDER = 2
SHORT_CONV = 3
N_BANDS = 16
POS_EMB_DIM = 1 + 2 * N_BANDS
FILTER_HIDDEN = 64
N_DIRS = 2
FILTER_CH = HYENA_ORDER * N_DIRS * HYENA_WIDTH
DECAY_TARGET = 1e-2
FAST_DECAY_PCT = 0.3
SLOW_DECAY_PCT = 1.5
MIX_WIDTH = ATTN_WIDTH + HYENA_WIDTH
IN_WIDTH = Q_LORA_RANK + KV_LORA_RANK + QK_ROPE_DIM + (HYENA_ORDER + 1) * HYENA_WIDTH
D_FF = 2816
N_EXPERTS = 8
TOP_K = 2
EXPERT_FF = 2816
N_DENSE = (DEPTH + 1) // 2
N_MOE = DEPTH // 2
EPS = 1e-6

kernel_name = 'hybrid_mla_hyena_moe_encoder'


def rms_norm(x, g):
    xf = x.astype(jnp.float32)
    y = xf * lax.rsqrt(jnp.mean(xf * xf, axis=-1, keepdims=True) + EPS)
    return (y * g.astype(jnp.float32)).astype(x.dtype)


def rope_tables(L):
    pos = jnp.arange(L, dtype=jnp.float32)
    inv = ROPE_THETA ** (-jnp.arange(0, QK_ROPE_DIM, 2, dtype=jnp.float32) / QK_ROPE_DIM)
    ang = pos[:, None] * inv[None, :]
    return jnp.cos(ang), jnp.sin(ang)


def apply_rope(x, cos, sin):
    half = QK_ROPE_DIM // 2
    xf = x.astype(jnp.float32)
    x1, x2 = xf[..., :half], xf[..., half:]
    c, s = cos[None, :, None, :], sin[None, :, None, :]
    return jnp.concatenate([x1 * c - x2 * s, x1 * s + x2 * c], axis=-1).astype(x.dtype)


def block_attention(q, k, v):
    B, L = q.shape[0], q.shape[1]
    nb = L // Q_BLOCK
    qb = q.reshape(B, nb, Q_BLOCK, N_HEADS, QK_DIM).transpose(1, 0, 2, 3, 4)
    scale = QK_DIM ** -0.5

    def one_block(qblk):
        s = jnp.einsum('bqhd,bkhd->bhqk', qblk, k, preferred_element_type=jnp.float32) * scale
        p = jax.nn.softmax(s, axis=-1)
        return jnp.einsum('bhqk,bkhd->bqhd', p.astype(v.dtype), v)

    o = lax.map(one_block, qb)
    return o.transpose(1, 0, 2, 3, 4).reshape(B, L, N_HEADS * V_HEAD_DIM)


def hyena_filters(L, w1, b1, freq, w2, b2, w3, decay):
    f32 = jnp.float32
    t = jnp.linspace(0.0, 1.0, L, dtype=f32)[:, None]
    w = 2.0 * math.pi * jnp.arange(L, dtype=f32) / L
    f = jnp.linspace(1e-4, N_BANDS - 1, N_BANDS, dtype=f32)
    ang = w[:, None] * f[None, :]
    z = jnp.concatenate([t, jnp.cos(ang), -jnp.sin(ang)], axis=-1)
    fr = freq.astype(f32)
    h = jnp.sin(fr * (z @ w1.astype(f32) + b1.astype(f32)))
    h = jnp.sin(fr * (h @ w2.astype(f32) + b2.astype(f32)))
    h = (h @ w3.astype(f32)) * jnp.exp(-t * jnp.abs(decay.astype(f32)))
    return h.reshape(L, HYENA_ORDER, N_DIRS, HYENA_WIDTH)


def bidir_long_conv(u, h_fwd, h_bwd, d_skip):
    L, C = u.shape[1], u.shape[2]
    n = 2 * L
    k = jnp.concatenate([h_fwd, jnp.zeros((1, C), jnp.float32), h_bwd[:0:-1]], axis=0)
    kf = jnp.fft.rfft(k, axis=0)
    uf = jnp.fft.rfft(u, n=n, axis=1)
    y = jnp.fft.irfft(uf * kf[None], n=n, axis=1)[:, :L]
    return y + u * d_skip.astype(jnp.float32)


def short_conv(x, w, b):
    L = x.shape[1]
    xp = jnp.pad(x, ((0, 0), (1, 1), (0, 0)))
    return xp[:, :L] * w[0] + xp[:, 1:L + 1] * w[1] + xp[:, 2:] * w[2] + b


def hyena_mixer(u, conv_w, conv_b, filt, d_skip):
    u = short_conv(u, conv_w, conv_b).astype(jnp.float32)
    v, x1, x2 = jnp.split(u, 3, axis=-1)
    z = v
    for n, gate in enumerate((x1, x2)):
        z = gate * bidir_long_conv(z, filt[:, n, 0], filt[:, n, 1], d_skip[n])
    return z


def token_mixer(h, w_in, cq_norm, ckv_norm, w_uq, w_ukv, q_norm, k_norm,
                conv_w, conv_b, filt, hyena_d, out_norm, w_out, cos, sin):
    B, L = h.shape[0], h.shape[1]
    proj = h @ w_in
    c_q, c_kv, k_rope, u_hy = jnp.split(
        proj, [Q_LORA_RANK, Q_LORA_RANK + KV_LORA_RANK, Q_LORA_RANK + KV_LORA_RANK + QK_ROPE_DIM], axis=-1)
    q = (rms_norm(c_q, cq_norm) @ w_uq).reshape(B, L, N_HEADS, QK_DIM)
    kv = (rms_norm(c_kv, ckv_norm) @ w_ukv).reshape(B, L, N_HEADS, QK_NOPE_DIM + V_HEAD_DIM)
    k_nope, v = kv[..., :QK_NOPE_DIM], kv[..., QK_NOPE_DIM:]
    k = jnp.concatenate(
        [k_nope, jnp.broadcast_to(k_rope[:, :, None, :], (B, L, N_HEADS, QK_ROPE_DIM))], axis=-1)
    q = rms_norm(q, q_norm)
    k = rms_norm(k, k_norm)
    q = jnp.concatenate([q[..., :QK_NOPE_DIM], apply_rope(q[..., QK_NOPE_DIM:], cos, sin)], axis=-1)
    k = jnp.concatenate([k[..., :QK_NOPE_DIM], apply_rope(k[..., QK_NOPE_DIM:], cos, sin)], axis=-1)
    attn = block_attention(q, k, v)
    hy = hyena_mixer(u_hy, conv_w, conv_b, filt, hyena_d).astype(h.dtype)
    merged = jnp.concatenate(
        [rms_norm(attn, out_norm[:ATTN_WIDTH]), rms_norm(hy, out_norm[ATTN_WIDTH:])], axis=-1)
    return merged @ w_out


def swiglu(x, wg, wu, wd):
    return (jax.nn.silu(x @ wg) * (x @ wu)) @ wd


def moe_swiglu(x, router_w, wg, wu, wd):
    logits = (x @ router_w).astype(jnp.float32)
    top_v, top_i = lax.top_k(logits, TOP_K)
    gates = jax.nn.softmax(top_v, axis=-1)
    combine = jnp.sum(jax.nn.one_hot(top_i, N_EXPERTS, dtype=jnp.float32) * gates[..., None], axis=-2)
    out = jnp.zeros(x.shape, jnp.float32)
    for e in range(N_EXPERTS):
        out = out + combine[..., e:e + 1] * swiglu(x, wg[e], wu[e], wd[e]).astype(jnp.float32)
    return out.astype(x.dtype)


def trunk(x, params):
    (mix_norm, w_in, cq_norm, ckv_norm, w_uq, w_ukv, q_norm, k_norm, conv_w, conv_b,
     filt_w1, filt_b1, filt_freq, filt_w2, filt_b2, filt_w3, filt_decay, hyena_d,
     out_norm, w_out, ffn_norm, dense_wg, dense_wu, dense_wd,
     router_w, moe_wg, moe_wu, moe_wd) = params
    L = x.shape[1]
    cos, sin = rope_tables(L)
    for layer in range(DEPTH):
        filt = hyena_filters(L, filt_w1[layer], filt_b1[layer], filt_freq[layer], filt_w2[layer],
                             filt_b2[layer], filt_w3[layer], filt_decay[layer])
        x = x + token_mixer(rms_norm(x, mix_norm[layer]), w_in[layer], cq_norm[layer], ckv_norm[layer],
                            w_uq[layer], w_ukv[layer], q_norm[layer], k_norm[layer],
                            conv_w[layer], conv_b[layer], filt, hyena_d[layer],
                            out_norm[layer], w_out[layer], cos, sin)
        hn = rms_norm(x, ffn_norm[layer])
        j = layer // 2
        if layer % 2 == 0:
            x = x + swiglu(hn, dense_wg[j], dense_wu[j], dense_wd[j])
        else:
            x = x + moe_swiglu(hn, router_w[j], moe_wg[j], moe_wu[j], moe_wd[j])
    return x


def setup_inputs(seed: int = 0) -> dict:
    key = jax.random.key(seed)
    ks = iter(jax.random.split(key, 40))
    f32 = jnp.float32

    def nrm(shape, scale):
        return jax.random.normal(next(ks), shape, f32) * scale

    def gain(shape):
        return 1.0 + 0.02 * jax.random.normal(next(ks), shape, f32)

    min_decay = -math.log(DECAY_TARGET) / SLOW_DECAY_PCT
    max_decay = -math.log(DECAY_TARGET) / FAST_DECAY_PCT
    base_decay = jnp.tile(jnp.linspace(min_decay, max_decay, HYENA_WIDTH, dtype=f32), HYENA_ORDER * N_DIRS)
    return {
        'x_prompt': nrm((BATCH, SEQ, D_MODEL), 1.0),
        'x_sample': nrm((DEC_BATCH, DEC_SEQ, D_MODEL), 1.0),
        'mix_norm': gain((DEPTH, D_MODEL)),
        'w_in': nrm((DEPTH, D_MODEL, IN_WIDTH), D_MODEL ** -0.5),
        'cq_norm': gain((DEPTH, Q_LORA_RANK)),
        'ckv_norm': gain((DEPTH, KV_LORA_RANK)),
        'w_uq': nrm((DEPTH, Q_LORA_RANK, N_HEADS * QK_DIM), Q_LORA_RANK ** -0.5),
        'w_ukv': nrm((DEPTH, KV_LORA_RANK, N_HEADS * (QK_NOPE_DIM + V_HEAD_DIM)), KV_LORA_RANK ** -0.5),
        'q_norm': gain((DEPTH, QK_DIM)),
        'k_norm': gain((DEPTH, QK_DIM)),
        'conv_w': nrm((DEPTH, SHORT_CONV, (HYENA_ORDER + 1) * HYENA_WIDTH), SHORT_CONV ** -0.5),
        'conv_b': nrm((DEPTH, (HYENA_ORDER + 1) * HYENA_WIDTH), 0.02),
        'filt_w1': nrm((DEPTH, POS_EMB_DIM, FILTER_HIDDEN), POS_EMB_DIM ** -0.5),
        'filt_b1': nrm((DEPTH, FILTER_HIDDEN), 0.02),
        'filt_freq': gain((DEPTH, FILTER_HIDDEN)),
        'filt_w2': nrm((DEPTH, FILTER_HIDDEN, FILTER_HIDDEN), FILTER_HIDDEN ** -0.5),
        'filt_b2': nrm((DEPTH, FILTER_HIDDEN), 0.02),
        'filt_w3': nrm((DEPTH, FILTER_HIDDEN, FILTER_CH), FILTER_HIDDEN ** -0.5),
        'filt_decay': base_decay[None, :] + nrm((DEPTH, FILTER_CH), 0.1),
        'hyena_d': nrm((DEPTH, HYENA_ORDER, HYENA_WIDTH), 1.0),
        'out_norm': gain((DEPTH, MIX_WIDTH)),
        'w_out': nrm((DEPTH, MIX_WIDTH, D_MODEL), MIX_WIDTH ** -0.5),
        'ffn_norm': gain((DEPTH, D_MODEL)),
        'dense_wg': nrm((N_DENSE, D_MODEL, D_FF), D_MODEL ** -0.5),
        'dense_wu': nrm((N_DENSE, D_MODEL, D_FF), D_MODEL ** -0.5),
        'dense_wd': nrm((N_DENSE, D_FF, D_MODEL), D_FF ** -0.5),
        'router_w': nrm((N_MOE, D_MODEL, N_EXPERTS), D_MODEL ** -0.5),
        'moe_wg': nrm((N_MOE, N_EXPERTS, D_MODEL, EXPERT_FF), D_MODEL ** -0.5),
        'moe_wu': nrm((N_MOE, N_EXPERTS, D_MODEL, EXPERT_FF), D_MODEL ** -0.5),
        'moe_wd': nrm((N_MOE, N_EXPERTS, EXPERT_FF, D_MODEL), EXPERT_FF ** -0.5),
    }


def reference(x_prompt, x_sample, mix_norm, w_in, cq_norm, ckv_norm, w_uq, w_ukv, q_norm, k_norm,
              conv_w, conv_b, filt_w1, filt_b1, filt_freq, filt_w2, filt_b2, filt_w3, filt_decay,
              hyena_d, out_norm, w_out, ffn_norm, dense_wg, dense_wu, dense_wd,
              router_w, moe_wg, moe_wu, moe_wd):
    params = (mix_norm, w_in, cq_norm, ckv_norm, w_uq, w_ukv, q_norm, k_norm, conv_w, conv_b,
              filt_w1, filt_b1, filt_freq, filt_w2, filt_b2, filt_w3, filt_decay, hyena_d,
              out_norm, w_out, ffn_norm, dense_wg, dense_wu, dense_wd,
              router_w, moe_wg, moe_wu, moe_wd)
    y_prompt = trunk(x_prompt, params)
    y_sample = trunk(x_sample, params)
    return (y_prompt, y_sample)
```

```python
import functools
import math

import jax
import jax.numpy as jnp
from jax import lax
from jax.experimental import pallas as pl
from jax.experimental.pallas import tpu as pltpu

BF = jnp.bfloat16
F32 = jnp.float32

D_MODEL = 1024
N_HEADS = 8
QK_NOPE = 64
QK_ROPE = 32
QK_DIM = QK_NOPE + QK_ROPE
V_DIM = 64
Q_LORA = 256
KV_LORA = 128
ATTN_W = N_HEADS * V_DIM
HY_W = D_MODEL - ATTN_W
ROPE_THETA = 10000.0
N_BANDS = 16
POS_EMB = 1 + 2 * N_BANDS
FILT_HID = 64
N_EXPERTS = 8
EPS = 1e-6

LANES = 128
HEAD_PAD = LANES
PAD_W = N_HEADS * HEAD_PAD
VMEM_LIMIT = 56 * 1024 * 1024
HI = lax.Precision.HIGHEST


def _cparams(sem):
    return pltpu.CompilerParams(dimension_semantics=sem, vmem_limit_bytes=VMEM_LIMIT)


def _rms(x, g):
    ms = jnp.mean(x * x, axis=-1, keepdims=True)
    return x * lax.rsqrt(ms + EPS) * g


def _dot(a, b):
    return jnp.dot(a, b, preferred_element_type=F32)


def _in_proj_kernel(x_ref, xp_ref, xn_ref, gmix_ref, wa_ref, why_ref, cqn_ref, ckvn_ref,
                    wuq_ref, wuk_ref, wuv_ref, pkr_ref, gq_ref, gk_ref,
                    cos_ref, sina_ref, sinb_ref, cw_ref, cb_ref,
                    q_ref, k_ref, v_ref, hv_ref, hx1_ref, hx2_ref, *, tiles_per_seq):
    i = pl.program_id(0)
    tm = x_ref.shape[0]
    g = gmix_ref[...]
    why = why_ref[...]
    h = _rms(x_ref[...], g).astype(BF)
    pa = _dot(h, wa_ref[...])
    u = _dot(h, why)
    up = _dot(_rms(xp_ref[...], g).astype(BF), why)
    un = _dot(_rms(xn_ref[...], g).astype(BF), why)
    pos = i % tiles_per_seq
    up_row = jnp.where(pos == 0, 0.0, up[7:8, :])
    un_row = jnp.where(pos == tiles_per_seq - 1, 0.0, un[0:1, :])
    row = lax.broadcasted_iota(jnp.int32, u.shape, 0)
    u_prev = jnp.where(row == 0, up_row, pltpu.roll(u, 1, 0))
    u_next = jnp.where(row == tm - 1, un_row, pltpu.roll(u, tm - 1, 0))
    cw = cw_ref[...]
    uc = u_prev * cw[0:1, :] + u * cw[1:2, :] + u_next * cw[2:3, :] + cb_ref[...]
    hv_ref[...] = uc[:, 0:HY_W].astype(BF)
    hx1_ref[...] = uc[:, HY_W:2 * HY_W].astype(BF)
    hx2_ref[...] = uc[:, 2 * HY_W:3 * HY_W].astype(BF)

    c_q = pa[:, 0:Q_LORA]
    c_kv = pa[:, Q_LORA:Q_LORA + KV_LORA]
    k_r = pa[:, Q_LORA + KV_LORA:Q_LORA + KV_LORA + QK_ROPE]
    cqn = _rms(c_q, cqn_ref[...]).astype(BF)
    ckvn = _rms(c_kv, ckvn_ref[...]).astype(BF)
    q = _dot(cqn, wuq_ref[...])
    k = _dot(ckvn, wuk_ref[...]) + _dot(k_r.astype(BF), pkr_ref[...])
    v_ref[...] = _dot(ckvn, wuv_ref[...]).astype(BF)
    cos = cos_ref[...]
    sa = sina_ref[...]
    sb = sinb_ref[...]
    half = QK_ROPE // 2
    scale = QK_DIM ** -0.5
    for hh in range(N_HEADS):
        sl = slice(HEAD_PAD * hh, HEAD_PAD * (hh + 1))
        for src, gref, oref, sc in ((q, gq_ref, q_ref, scale), (k, gk_ref, k_ref, 1.0)):
            t = src[:, sl]
            ms = jnp.sum(t * t, axis=-1, keepdims=True) * (1.0 / QK_DIM)
            tn = t * lax.rsqrt(ms + EPS) * gref[:, sl]
            tr = tn * cos + pltpu.roll(tn, HEAD_PAD - half, 1) * sa + pltpu.roll(tn, half, 1) * sb
            oref[:, sl] = (tr * sc).astype(BF)


def _in_proj(x2d, L, lw, rope, tm):
    T = x2d.shape[0]
    nt = T // tm
    r8 = tm // 8
    full = lambda a: pl.BlockSpec(a.shape, lambda i: (0,) * a.ndim)
    tps = L // tm
    weights = [lw['mix_norm'], lw['w_a'], lw['w_hy'], lw['cq_norm'], lw['ckv_norm'],
               lw['w_uq'], lw['w_uk'], lw['w_uv'], lw['p_kr'], lw['gq'], lw['gk']]
    in_specs = [
        pl.BlockSpec((tm, D_MODEL), lambda i: (i, 0)),
        pl.BlockSpec((8, D_MODEL), lambda i: (jnp.maximum(i * r8 - 1, 0), 0)),
        pl.BlockSpec((8, D_MODEL), lambda i: (jnp.minimum((i + 1) * r8, T // 8 - 1), 0)),
    ] + [full(w) for w in weights] + [
        pl.BlockSpec((tm, LANES), lambda i: (i % tps, 0)),
        pl.BlockSpec((tm, LANES), lambda i: (i % tps, 0)),
        pl.BlockSpec((tm, LANES), lambda i: (i % tps, 0)),
        full(lw['conv_w']), full(lw['conv_b']),
    ]
    out_shape = [jax.ShapeDtypeStruct((T, PAD_W), BF)] * 3 + [jax.ShapeDtypeStruct((T, HY_W), BF)] * 3
    out_specs = [pl.BlockSpec((tm, PAD_W), lambda i: (i, 0))] * 3 + [pl.BlockSpec((tm, HY_W), lambda i: (i, 0))] * 3
    return pl.pallas_call(
        functools.partial(_in_proj_kernel, tiles_per_seq=tps),
        out_shape=out_shape, grid=(nt,), in_specs=in_specs, out_specs=out_specs,
        compiler_params=_cparams(("parallel",)), name="in_proj",
    )(x2d, x2d, x2d, *weights, *rope, lw['conv_w'], lw['conv_b'])


def _flash_kernel(q_ref, k_ref, v_ref, o_ref, m_sc, l_sc, acc_sc):
    ki = pl.program_id(3)

    @pl.when(ki == 0)
    def _():
        m_sc[...] = jnp.full(m_sc.shape, -jnp.inf, F32)
        l_sc[...] = jnp.zeros(l_sc.shape, F32)
        acc_sc[...] = jnp.zeros(acc_sc.shape, F32)

    s = lax.dot_general(q_ref[...], k_ref[...], (((1,), (1,)), ((), ())), preferred_element_type=F32)
    m_prev = m_sc[...]
    m_new = jnp.maximum(m_prev, jnp.max(s, axis=-1, keepdims=True))
    alpha = jnp.exp(m_prev - m_new)
    p = jnp.exp(s - m_new)
    l_sc[...] = alpha * l_sc[...] + jnp.sum(p, axis=-1, keepdims=True)
    acc_sc[...] = alpha * acc_sc[...] + _dot(p.astype(BF), v_ref[...])
    m_sc[...] = m_new

    @pl.when(ki == pl.num_programs(3) - 1)
    def _():
        o_ref[...] = (acc_sc[...] / l_sc[...]).astype(o_ref.dtype)


def _flash(q, k, v, B, L, tq, tk):
    nq, nk = L // tq, L // tk
    return pl.pallas_call(
        _flash_kernel,
        out_shape=jax.ShapeDtypeStruct(q.shape, BF),
        grid=(B, N_HEADS, nq, nk),
        in_specs=[
            pl.BlockSpec((tq, HEAD_PAD), lambda b, h, qi, ki: (b * nq + qi, h)),
            pl.BlockSpec((tk, HEAD_PAD), lambda b, h, qi, ki: (b * nk + ki, h)),
            pl.BlockSpec((tk, HEAD_PAD), lambda b, h, qi, ki: (b * nk + ki, h)),
        ],
        out_specs=pl.BlockSpec((tq, HEAD_PAD), lambda b, h, qi, ki: (b * nq + qi, h)),
        scratch_shapes=[pltpu.VMEM((tq, 1), F32), pltpu.VMEM((tq, 1), F32), pltpu.VMEM((tq, HEAD_PAD), F32)],
        compiler_params=_cparams(("parallel", "parallel", "parallel", "arbitrary")), name="flash",
    )(q, k, v)


def _filter_kernel(z_ref, w1_ref, b1_ref, fr_ref, w2_ref, b2_ref, w3_ref, dec_ref, o_ref):
    z = z_ref[...]
    fr = fr_ref[...]
    h = jnp.sin(fr * (jnp.dot(z, w1_ref[...], precision=HI, preferred_element_type=F32) + b1_ref[...]))
    h = jnp.sin(fr * (jnp.dot(h, w2_ref[...], precision=HI, preferred_element_type=F32) + b2_ref[...]))
    t = z[:, 0:1]
    o = jnp.dot(h, w3_ref[...], precision=HI, preferred_element_type=F32)
    o_ref[...] = o * jnp.exp(-t * jnp.abs(dec_ref[...]))


def _filters(zfeat, lw, tl):
    L = zfeat.shape[0]
    ch = lw['filt_w3'].shape[1]
    ws = [lw['filt_w1'], lw['filt_b1'], lw['filt_freq'], lw['filt_w2'], lw['filt_b2'], lw['filt_w3'],
          lw['filt_decay']]
    full = lambda a: pl.BlockSpec(a.shape, lambda i: (0,) * a.ndim)
    return pl.pallas_call(
        _filter_kernel,
        out_shape=jax.ShapeDtypeStruct((L, ch), F32),
        grid=(L // tl,),
        in_specs=[pl.BlockSpec((tl, LANES), lambda i: (i, 0))] + [full(w) for w in ws],
        out_specs=pl.BlockSpec((tl, ch), lambda i: (i, 0)),
        compiler_params=_cparams(("parallel",)), name="filt_mlp",
    )(zfeat, *ws)


def _dft_rows_kernel(x_ref, m_ref, o_ref):
    n1 = o_ref.shape[1]
    r = _dot(m_ref[...], x_ref[...].astype(BF))
    o_ref[0] = r[:n1].astype(o_ref.dtype)
    o_ref[1] = r[n1:].astype(o_ref.dtype)


def _dft_rows(xv, m, cb):
    P, n1, cols = xv.shape
    return pl.pallas_call(
        _dft_rows_kernel,
        out_shape=jax.ShapeDtypeStruct((P, 2, n1, cols), BF),
        grid=(P, cols // cb),
        in_specs=[pl.BlockSpec((None, n1, cb), lambda p, j: (p, 0, j)),
                  pl.BlockSpec(m.shape, lambda p, j: (0, 0))],
        out_specs=pl.BlockSpec((None, 2, n1, cb), lambda p, j: (p, 0, 0, j)),
        compiler_params=_cparams(("parallel", "parallel")), name="dft_rows",
    )(xv, m)


def _lane_tile(t, width):
    reps = width // t.shape[-1]
    return t if reps == 1 else jnp.concatenate([t] * reps, axis=-1)


def _filt_spec_kernel(a_ref, twr_ref, twi_ref, gf_ref, kr_ref, ki_ref, *, scale):
    kb, n2, c = kr_ref.shape
    gf = gf_ref[...]
    for j in range(kb):
        twr = _lane_tile(twr_ref[j], c)
        twi = _lane_tile(twi_ref[j], c)
        ar = a_ref[0, j].astype(F32)
        ai = a_ref[1, j].astype(F32)
        br = ar * twr - ai * twi
        bi = ar * twi + ai * twr
        x = _dot(gf, jnp.concatenate([br, bi], axis=0).astype(BF))
        kr_ref[j] = x[:n2] * scale
        ki_ref[j] = x[n2:] * scale


def _filt_spec(a5, twr, twi, gf, kb, scale):
    _, _, n1, n2, c = a5.shape
    return pl.pallas_call(
        functools.partial(_filt_spec_kernel, scale=scale),
        out_shape=[jax.ShapeDtypeStruct((n1, n2, c), F32)] * 2,
        grid=(n1 // kb,),
        in_specs=[pl.BlockSpec((None, 2, kb, n2, c), lambda i: (0, 0, i, 0, 0)),
                  pl.BlockSpec((kb, n2, LANES), lambda i: (i, 0, 0)),
                  pl.BlockSpec((kb, n2, LANES), lambda i: (i, 0, 0)),
                  pl.BlockSpec(gf.shape, lambda i: (0, 0))],
        out_specs=[pl.BlockSpec((kb, n2, c), lambda i: (i, 0, 0))] * 2,
        compiler_params=_cparams(("parallel",)), name="filt_spec",
    )(a5, twr, twi, gf)


def _spec_kernel(a_ref, twr_ref, twi_ref, kr_ref, ki_ref, gf_ref, gi_ref, o_ref):
    kb, n2, c = kr_ref.shape
    gf = gf_ref[...]
    gi = gi_ref[...]
    for j in range(kb):
        twr = _lane_tile(twr_ref[j], c)
        twi = _lane_tile(twi_ref[j], c)
        ar = a_ref[0, j].astype(F32)
        ai = a_ref[1, j].astype(F32)
        br = ar * twr - ai * twi
        bi = ar * twi + ai * twr
        x = _dot(gf, jnp.concatenate([br, bi], axis=0).astype(BF))
        xr, xi = x[:n2], x[n2:]
        kr = kr_ref[j]
        ki = ki_ref[j]
        zr = xr * kr - xi * ki
        zi = xr * ki + xi * kr
        y = _dot(gi, jnp.concatenate([zr, zi], axis=0).astype(BF))
        yr, yi = y[:n2], y[n2:]
        o_ref[0, j] = (yr * twr + yi * twi).astype(o_ref.dtype)
        o_ref[1, j] = (yi * twr - yr * twi).astype(o_ref.dtype)


def _spec(a5, twr, twi, kr, ki, gf, gi, kb):
    P, _, n1, n2, c = a5.shape
    return pl.pallas_call(
        _spec_kernel,
        out_shape=jax.ShapeDtypeStruct(a5.shape, BF),
        grid=(n1 // kb, P),
        in_specs=[pl.BlockSpec((None, 2, kb, n2, c), lambda i, p: (p, 0, i, 0, 0)),
                  pl.BlockSpec((kb, n2, LANES), lambda i, p: (i, 0, 0)),
                  pl.BlockSpec((kb, n2, LANES), lambda i, p: (i, 0, 0)),
                  pl.BlockSpec((kb, n2, c), lambda i, p: (i, 0, 0)),
                  pl.BlockSpec((kb, n2, c), lambda i, p: (i, 0, 0)),
                  pl.BlockSpec(gf.shape, lambda i, p: (0, 0)),
                  pl.BlockSpec(gi.shape, lambda i, p: (0, 0))],
        out_specs=pl.BlockSpec((None, 2, kb, n2, c), lambda i, p: (p, 0, i, 0, 0)),
        compiler_params=_cparams(("parallel", "parallel")), name="spec",
    )(a5, twr, twi, kr, ki, gf, gi)


def _idft_gate_kernel(b_ref, m_ref, u_ref, g_ref, d_ref, o_ref):
    _, n1, cb = b_ref.shape
    y = _dot(m_ref[...], b_ref[...].reshape(2 * n1, cb))
    u = u_ref[...].astype(F32)
    o_ref[...] = (g_ref[...].astype(F32) * (y + d_ref[...] * u)).astype(o_ref.dtype)


def _idft_gate(b4, m, uv, gv, d_t, cb):
    P, _, n1, cols = b4.shape
    return pl.pallas_call(
        _idft_gate_kernel,
        out_shape=jax.ShapeDtypeStruct((P, n1, cols), BF),
        grid=(P, cols // cb),
        in_specs=[pl.BlockSpec((None, 2, n1, cb), lambda p, j: (p, 0, 0, j)),
                  pl.BlockSpec(m.shape, lambda p, j: (0, 0)),
                  pl.BlockSpec((None, n1, cb), lambda p, j: (p, 0, j)),
                  pl.BlockSpec((None, n1, cb), lambda p, j: (p, 0, j)),
                  pl.BlockSpec((1, cb), lambda p, j: (0, 0))],
        out_specs=pl.BlockSpec((None, n1, cb), lambda p, j: (p, 0, j)),
        compiler_params=_cparams(("parallel", "parallel")), name="idft_gate",
    )(b4, m, uv, gv, d_t)


def _fft_plan(L):
    n = 2 * L
    n1 = 1 << ((n.bit_length() - 1 + 1) // 2)
    n2 = n // n1
    i1 = jnp.arange(n1, dtype=jnp.int32)
    i2 = jnp.arange(n2, dtype=jnp.int32)
    a1 = (2.0 * math.pi / n1) * ((i1[:, None] * i1[None, :]) % n1).astype(F32)
    c1, s1 = jnp.cos(a1), jnp.sin(a1)
    hn = n1 // 2
    m_sig = jnp.concatenate([jnp.concatenate([c1[:, :hn], s1[:, :hn]], 1),
                             jnp.concatenate([-s1[:, :hn], c1[:, :hn]], 1)], 0).astype(BF)
    m_filt = jnp.concatenate([c1, -s1], 0).astype(BF)
    m_inv = jnp.concatenate([jnp.concatenate([c1[:hn], -s1[:hn]], 1),
                             jnp.concatenate([s1[:hn], c1[:hn]], 1)], 0).astype(BF)
    a2 = (2.0 * math.pi / n2) * ((i2[:, None] * i2[None, :]) % n2).astype(F32)
    c2, s2 = jnp.cos(a2), jnp.sin(a2)
    gf = jnp.concatenate([jnp.concatenate([c2, s2], 1), jnp.concatenate([-s2, c2], 1)], 0).astype(BF)
    gi = jnp.concatenate([jnp.concatenate([c2, -s2], 1), jnp.concatenate([s2, c2], 1)], 0).astype(BF)
    at = (2.0 * math.pi / n) * (i1[:, None] * i2[None, :]).astype(F32)
    twr = jnp.broadcast_to(jnp.cos(at)[:, :, None], (n1, n2, LANES))
    twi = jnp.broadcast_to(-jnp.sin(at)[:, :, None], (n1, n2, LANES))
    return dict(n=n, n1=n1, n2=n2, m_sig=m_sig, m_filt=m_filt, m_inv=m_inv, gf=gf, gi=gi, twr=twr, twi=twi)


def _pick(total, want):
    b = min(total, want)
    while total % b:
        b //= 2
    return b


def _filter_spectra(filt, plan, L):
    n1, n2 = plan['n1'], plan['n2']
    f4 = filt.reshape(L, 2, 2, HY_W)
    out = []
    for o in range(2):
        kern = jnp.concatenate([f4[:, o, 0], jnp.zeros((1, HY_W), F32), f4[:0:-1, o, 1]], axis=0)
        cols = n2 * HY_W
        a = _dft_rows(kern.reshape(1, n1, cols), plan['m_filt'], _pick(cols, 4096))
        a5 = a.reshape(1, 2, n1, n2, HY_W)
        out.append(_filt_spec(a5, plan['twr'], plan['twi'], plan['gf'], _pick(n1, 8), 1.0 / plan['n']))
    return out


def _long_conv_gate(u, gate, d, kspec, plan, B, L):
    n1, n2 = plan['n1'], plan['n2']
    P = B // 2
    cols = n2 * HY_W
    cb = _pick(cols, 4096)
    uv = u.reshape(P, n1, cols)
    gv = gate.reshape(P, n1, cols)
    a = _dft_rows(uv, plan['m_sig'], cb)
    b = _spec(a.reshape(P, 2, n1, n2, HY_W), plan['twr'], plan['twi'], kspec[0], kspec[1],
              plan['gf'], plan['gi'], _pick(n1, 8))
    d_t = jnp.tile(d.reshape(1, HY_W).astype(F32), (1, cb // HY_W))
    z = _idft_gate(b.reshape(P, 2, n1, cols), plan['m_inv'], uv, gv, d_t, cb)
    return z.reshape(B * L, HY_W)


def _out_proj_kernel(x_ref, a_ref, hy_ref, ga_ref, gh_ref, wa_ref, wh_ref, o_ref):
    a = a_ref[...].astype(F32)
    hy = hy_ref[...].astype(F32)
    ra = lax.rsqrt(jnp.sum(a * a, axis=-1, keepdims=True) * (1.0 / ATTN_W) + EPS)
    rh = lax.rsqrt(jnp.sum(hy * hy, axis=-1, keepdims=True) * (1.0 / HY_W) + EPS)
    an = (a * ra * ga_ref[...]).astype(BF)
    hn = (hy * rh * gh_ref[...]).astype(BF)
    o_ref[...] = x_ref[...] + _dot(an, wa_ref[...]) + _dot(hn, wh_ref[...])


def _out_proj(x2d, attn, hy, lw, tm):
    T = x2d.shape[0]
    full = lambda a: pl.BlockSpec(a.shape, lambda i: (0,) * a.ndim)
    ws = [lw['ga'], lw['gh'], lw['w_out_a'], lw['w_out_h']]
    return pl.pallas_call(
        _out_proj_kernel,
        out_shape=jax.ShapeDtypeStruct((T, D_MODEL), F32),
        grid=(T // tm,),
        in_specs=[pl.BlockSpec((tm, D_MODEL), lambda i: (i, 0)),
                  pl.BlockSpec((tm, PAD_W), lambda i: (i, 0)),
                  pl.BlockSpec((tm, HY_W), lambda i: (i, 0))] + [full(w) for w in ws],
        out_specs=pl.BlockSpec((tm, D_MODEL), lambda i: (i, 0)),
        compiler_params=_cparams(("parallel",)), name="out_proj",
    )(x2d, attn, hy, *ws)


def _silu_mul(g, u):
    return g * (1.0 / (1.0 + jnp.exp(-g))) * u


def _ffn_kernel(x_ref, g_ref, wg_ref, wu_ref, wd_ref, o_ref, hn_sc, acc_sc):
    j = pl.program_id(1)

    @pl.when(j == 0)
    def _():
        hn_sc[...] = _rms(x_ref[...], g_ref[...]).astype(BF)
        acc_sc[...] = jnp.zeros(acc_sc.shape, F32)

    hn = hn_sc[...]
    a = _silu_mul(_dot(hn, wg_ref[...]), _dot(hn, wu_ref[...])).astype(BF)
    acc_sc[...] += _dot(a, wd_ref[...])

    @pl.when(j == pl.num_programs(1) - 1)
    def _():
        o_ref[...] = x_ref[...] + acc_sc[...]


def _ffn(x2d, g, wg, wu, wd, tm, tf):
    T = x2d.shape[0]
    ff = wg.shape[1]
    return pl.pallas_call(
        _ffn_kernel,
        out_shape=jax.ShapeDtypeStruct((T, D_MODEL), F32),
        grid=(T // tm, ff // tf),
        in_specs=[pl.BlockSpec((tm, D_MODEL), lambda i, j: (i, 0)),
                  pl.BlockSpec((1, D_MODEL), lambda i, j: (0, 0)),
                  pl.BlockSpec((D_MODEL, tf), lambda i, j: (0, j)),
                  pl.BlockSpec((D_MODEL, tf), lambda i, j: (0, j)),
                  pl.BlockSpec((tf, D_MODEL), lambda i, j: (j, 0))],
        out_specs=pl.BlockSpec((tm, D_MODEL), lambda i, j: (i, 0)),
        scratch_shapes=[pltpu.VMEM((tm, D_MODEL), BF), pltpu.VMEM((tm, D_MODEL), F32)],
        compiler_params=_cparams(("parallel", "arbitrary")), name="ffn",
    )(x2d, g, wg, wu, wd)


def _moe_kernel(x_ref, g_ref, rw_ref, wg_ref, wu_ref, wd_ref, o_ref, hn_sc, comb_sc, acc_sc):
    e = pl.program_id(1)
    j = pl.program_id(2)
    lane = lax.broadcasted_iota(jnp.int32, comb_sc.shape, 1)

    @pl.when((e == 0) & (j == 0))
    def _():
        hn = _rms(x_ref[...], g_ref[...])
        hn_sc[...] = hn.astype(BF)
        logits = jnp.dot(hn, rw_ref[...], precision=HI, preferred_element_type=F32)
        lg = jnp.where(lane < N_EXPERTS, logits, -jnp.inf)
        m1 = jnp.max(lg, axis=-1, keepdims=True)
        i1 = jnp.min(jnp.where(lg == m1, lane, LANES), axis=-1, keepdims=True)
        lg2 = jnp.where(lane == i1, -jnp.inf, lg)
        m2 = jnp.max(lg2, axis=-1, keepdims=True)
        i2 = jnp.min(jnp.where(lg2 == m2, lane, LANES), axis=-1, keepdims=True)
        e2 = jnp.exp(m2 - m1)
        g1 = 1.0 / (1.0 + e2)
        comb_sc[...] = jnp.where(lane == i1, g1, 0.0) + jnp.where(lane == i2, e2 * g1, 0.0)
        acc_sc[...] = jnp.zeros(acc_sc.shape, F32)

    w_e = jnp.sum(jnp.where(lane == e, comb_sc[...], 0.0), axis=-1, keepdims=True)
    hn = hn_sc[...]
    a = (_silu_mul(_dot(hn, wg_ref[...]), _dot(hn, wu_ref[...])) * w_e).astype(BF)
    acc_sc[...] += _dot(a, wd_ref[...])

    @pl.when((e == pl.num_programs(1) - 1) & (j == pl.num_programs(2) - 1))
    def _():
        o_ref[...] = x_ref[...] + acc_sc[...]


def _moe(x2d, g, rw, wg, wu, wd, tm, tf):
    T = x2d.shape[0]
    ne, _, ff = wg.shape
    return pl.pallas_call(
        _moe_kernel,
        out_shape=jax.ShapeDtypeStruct((T, D_MODEL), F32),
        grid=(T // tm, ne, ff // tf),
        in_specs=[pl.BlockSpec((tm, D_MODEL), lambda i, e, j: (i, 0)),
                  pl.BlockSpec((1, D_MODEL), lambda i, e, j: (0, 0)),
                  pl.BlockSpec((D_MODEL, LANES), lambda i, e, j: (0, 0)),
                  pl.BlockSpec((None, D_MODEL, tf), lambda i, e, j: (e, 0, j)),
                  pl.BlockSpec((None, D_MODEL, tf), lambda i, e, j: (e, 0, j)),
                  pl.BlockSpec((None, tf, D_MODEL), lambda i, e, j: (e, j, 0))],
        out_specs=pl.BlockSpec((tm, D_MODEL), lambda i, e, j: (i, 0)),
        scratch_shapes=[pltpu.VMEM((tm, D_MODEL), BF), pltpu.VMEM((tm, LANES), F32),
                        pltpu.VMEM((tm, D_MODEL), F32)],
        compiler_params=_cparams(("parallel", "arbitrary", "arbitrary")), name="moe",
    )(x2d, g, rw, wg, wu, wd)


def _pad_heads(w, d):
    lead = w.shape[:-1]
    w = w.reshape(lead + (N_HEADS, d))
    w = jnp.pad(w, [(0, 0)] * len(lead) + [(0, 0), (0, HEAD_PAD - d)])
    return w.reshape(lead + (PAD_W,))


def _layer_weights(p, layer):
    (mix_norm, w_in, cq_norm, ckv_norm, w_uq, w_ukv, q_norm, k_norm, conv_w, conv_b,
     filt_w1, filt_b1, filt_freq, filt_w2, filt_b2, filt_w3, filt_decay, hyena_d,
     out_norm, w_out, ffn_norm) = [a[layer] for a in p]
    na = Q_LORA + KV_LORA + QK_ROPE
    ukv = w_ukv.reshape(KV_LORA, N_HEADS, QK_NOPE + V_DIM)
    eye = jnp.eye(QK_ROPE, dtype=F32)
    p_kr = jnp.pad(eye, ((0, 0), (QK_NOPE, HEAD_PAD - QK_DIM)))
    row = lambda a: a.reshape(1, -1).astype(F32)
    w_out_a = w_out[:ATTN_W].reshape(N_HEADS, V_DIM, D_MODEL)
    w_out_a = jnp.pad(w_out_a, ((0, 0), (0, HEAD_PAD - V_DIM), (0, 0))).reshape(PAD_W, D_MODEL)
    return dict(
        mix_norm=row(mix_norm), w_a=w_in[:, :na].astype(BF), w_hy=w_in[:, na:].astype(BF),
        cq_norm=row(cq_norm), ckv_norm=row(ckv_norm),
        w_uq=_pad_heads(w_uq, QK_DIM).astype(BF),
        w_uk=_pad_heads(ukv[:, :, :QK_NOPE].reshape(KV_LORA, -1), QK_NOPE).astype(BF),
        w_uv=_pad_heads(ukv[:, :, QK_NOPE:].reshape(KV_LORA, -1), V_DIM).astype(BF),
        p_kr=jnp.tile(p_kr, (1, N_HEADS)).astype(BF),
        gq=jnp.tile(jnp.pad(q_norm, (0, HEAD_PAD - QK_DIM)), N_HEADS).reshape(1, PAD_W),
        gk=jnp.tile(jnp.pad(k_norm, (0, HEAD_PAD - QK_DIM)), N_HEADS).reshape(1, PAD_W),
        conv_w=conv_w.astype(F32), conv_b=row(conv_b),
        filt_w1=jnp.pad(filt_w1, ((0, LANES - POS_EMB), (0, 0))), filt_b1=row(filt_b1),
        filt_freq=row(filt_freq), filt_w2=filt_w2, filt_b2=row(filt_b2), filt_w3=filt_w3,
        filt_decay=row(filt_decay), hyena_d=hyena_d,
        ga=_pad_heads(out_norm[:ATTN_W], V_DIM).reshape(1, PAD_W), gh=row(out_norm[ATTN_W:]),
        w_out_a=w_out_a.astype(BF), w_out_h=w_out[ATTN_W:].astype(BF),
        ffn_norm=row(ffn_norm),
    )


def _position_tables(L):
    pos = jnp.arange(L, dtype=F32)
    half = QK_ROPE // 2
    inv = ROPE_THETA ** (-jnp.arange(0, QK_ROPE, 2, dtype=F32) / QK_ROPE)
    ang = pos[:, None] * inv[None, :]
    cos, sin = jnp.cos(ang), jnp.sin(ang)
    zeros = lambda w: jnp.zeros((L, w), F32)
    cos_t = jnp.concatenate([jnp.ones((L, QK_NOPE), F32), cos, cos, zeros(HEAD_PAD - QK_DIM)], 1)
    sin_a = jnp.concatenate([zeros(QK_NOPE), -sin, zeros(HEAD_PAD - QK_NOPE - half)], 1)
    sin_b = jnp.concatenate([zeros(QK_NOPE + half), sin, zeros(HEAD_PAD - QK_DIM)], 1)
    t = jnp.linspace(0.0, 1.0, L, dtype=F32)[:, None]
    w = 2.0 * math.pi * jnp.arange(L, dtype=F32) / L
    f = jnp.linspace(1e-4, N_BANDS - 1, N_BANDS, dtype=F32)
    a = w[:, None] * f[None, :]
    z = jnp.concatenate([t, jnp.cos(a), -jnp.sin(a), jnp.zeros((L, LANES - POS_EMB), F32)], axis=-1)
    return (cos_t, sin_a, sin_b), z


def _trunk(x, layers, mixers, plan, rope, zfeat):
    B, L, _ = x.shape
    T = B * L
    x2d = x.reshape(T, D_MODEL)
    tm = _pick(L, 512)
    for lw, mixer in zip(layers, mixers):
        q, k, v, hv, hx1, hx2 = _in_proj(x2d, L, lw, rope, tm)
        attn = _flash(q, k, v, B, L, _pick(L, 512), _pick(L, 1024))
        filt = _filters(zfeat, lw, _pick(L, 512))
        ks = _filter_spectra(filt, plan, L)
        z1 = _long_conv_gate(hv, hx1, lw['hyena_d'][0], ks[0], plan, B, L)
        hy = _long_conv_gate(z1, hx2, lw['hyena_d'][1], ks[1], plan, B, L)
        x2d = _out_proj(x2d, attn, hy, lw, tm)
        tf = 1408
        if mixer[0] == 'dense':
            _, wg, wu, wd = mixer
            x2d = _ffn(x2d, lw['ffn_norm'], wg, wu, wd, _pick(T, 1024), tf)
        else:
            _, rw, wg, wu, wd = mixer
            x2d = _moe(x2d, lw['ffn_norm'], rw, wg, wu, wd, _pick(T, 1024), tf)
    return x2d.reshape(B, L, D_MODEL)


def kernel(x_prompt, x_sample, mix_norm, w_in, cq_norm, ckv_norm, w_uq, w_ukv, q_norm, k_norm, conv_w, conv_b, filt_w1, filt_b1, filt_freq, filt_w2, filt_b2, filt_w3, filt_decay, hyena_d, out_norm, w_out, ffn_norm, dense_wg, dense_wu, dense_wd, router_w, moe_wg, moe_wu, moe_wd):
    per_layer = (mix_norm, w_in, cq_norm, ckv_norm, w_uq, w_ukv, q_norm, k_norm, conv_w, conv_b,
                 filt_w1, filt_b1, filt_freq, filt_w2, filt_b2, filt_w3, filt_decay, hyena_d,
                 out_norm, w_out, ffn_norm)
    depth = mix_norm.shape[0]
    layers = [_layer_weights(per_layer, l) for l in range(depth)]
    mixers = []
    for l in range(depth):
        j = l // 2
        if l % 2 == 0:
            mixers.append(('dense', dense_wg[j].astype(BF), dense_wu[j].astype(BF), dense_wd[j].astype(BF)))
        else:
            rw = jnp.pad(router_w[j], ((0, 0), (0, LANES - N_EXPERTS)))
            mixers.append(('moe', rw, moe_wg[j].astype(BF), moe_wu[j].astype(BF), moe_wd[j].astype(BF)))
    outs = []
    for x in (x_prompt, x_sample):
        L = x.shape[1]
        rope, zfeat = _position_tables(L)
        outs.append(_trunk(x, layers, mixers, _fft_plan(L), rope, zfeat))
    return tuple(outs)
```

```python
import functools
import math

import jax
import jax.numpy as jnp
from jax import lax
from jax.experimental import pallas as pl
from jax.experimental.pallas import tpu as pltpu

BF = jnp.bfloat16
F32 = jnp.float32

D_MODEL = 1024
N_HEADS = 8
QK_NOPE = 64
QK_ROPE = 32
QK_DIM = QK_NOPE + QK_ROPE
V_DIM = 64
Q_LORA = 256
KV_LORA = 128
ATTN_W = N_HEADS * V_DIM
HY_W = D_MODEL - ATTN_W
ROPE_THETA = 10000.0
N_BANDS = 16
POS_EMB = 1 + 2 * N_BANDS
FILT_HID = 64
N_EXPERTS = 8
EPS = 1e-6

LANES = 128
HEAD_PAD = LANES
PAD_W = N_HEADS * HEAD_PAD
VMEM_LIMIT = 56 * 1024 * 1024
HI = lax.Precision.HIGHEST
ATTN_TQ = 256
ATTN_KV_BYTES = 2 * 1024 * 1024


def _cparams(sem):
    return pltpu.CompilerParams(dimension_semantics=sem, vmem_limit_bytes=VMEM_LIMIT)


def _rms(x, g):
    ms = jnp.mean(x * x, axis=-1, keepdims=True)
    return x * lax.rsqrt(ms + EPS) * g


def _dot(a, b):
    return jnp.dot(a, b, preferred_element_type=F32)


def _in_proj_kernel(x_ref, xp_ref, xn_ref, gmix_ref, wa_ref, why_ref, cqn_ref, ckvn_ref,
                    wuq_ref, wuk_ref, wuv_ref, pkr_ref, gq_ref, gk_ref,
                    cos_ref, sina_ref, sinb_ref, cw_ref, cb_ref,
                    q_ref, k_ref, v_ref, hv_ref, hx1_ref, hx2_ref, *, tiles_per_seq):
    i = pl.program_id(0)
    tm = x_ref.shape[0]
    g = gmix_ref[...]
    why = why_ref[...]
    h = _rms(x_ref[...], g).astype(BF)
    pa = _dot(h, wa_ref[...])
    u = _dot(h, why)
    up = _dot(_rms(xp_ref[...], g).astype(BF), why)
    un = _dot(_rms(xn_ref[...], g).astype(BF), why)
    pos = i % tiles_per_seq
    up_row = jnp.where(pos == 0, 0.0, up[7:8, :])
    un_row = jnp.where(pos == tiles_per_seq - 1, 0.0, un[0:1, :])
    row = lax.broadcasted_iota(jnp.int32, u.shape, 0)
    u_prev = jnp.where(row == 0, up_row, pltpu.roll(u, 1, 0))
    u_next = jnp.where(row == tm - 1, un_row, pltpu.roll(u, tm - 1, 0))
    cw = cw_ref[...]
    uc = u_prev * cw[0:1, :] + u * cw[1:2, :] + u_next * cw[2:3, :] + cb_ref[...]
    hv_ref[...] = uc[:, 0:HY_W].astype(BF)
    hx1_ref[...] = uc[:, HY_W:2 * HY_W].astype(BF)
    hx2_ref[...] = uc[:, 2 * HY_W:3 * HY_W].astype(BF)

    c_q = pa[:, 0:Q_LORA]
    c_kv = pa[:, Q_LORA:Q_LORA + KV_LORA]
    k_r = pa[:, Q_LORA + KV_LORA:Q_LORA + KV_LORA + QK_ROPE]
    cqn = _rms(c_q, cqn_ref[...]).astype(BF)
    ckvn = _rms(c_kv, ckvn_ref[...]).astype(BF)
    q = _dot(cqn, wuq_ref[...])
    k = _dot(ckvn, wuk_ref[...]) + _dot(k_r.astype(BF), pkr_ref[...])
    lane = lax.broadcasted_iota(jnp.int32, (1, PAD_W), 1)
    ones_col = jnp.where(lane % HEAD_PAD == V_DIM, 1.0, 0.0)
    v_ref[...] = (_dot(ckvn, wuv_ref[...]) + ones_col).astype(BF)
    cos = cos_ref[...]
    sa = sina_ref[...]
    sb = sinb_ref[...]
    half = QK_ROPE // 2
    scale = QK_DIM ** -0.5 * math.log2(math.e)
    for hh in range(N_HEADS):
        sl = slice(HEAD_PAD * hh, HEAD_PAD * (hh + 1))
        for src, gref, oref, sc in ((q, gq_ref, q_ref, scale), (k, gk_ref, k_ref, 1.0)):
            t = src[:, sl]
            ms = jnp.sum(t * t, axis=-1, keepdims=True) * (1.0 / QK_DIM)
            tn = t * lax.rsqrt(ms + EPS) * gref[:, sl]
            tr = tn * cos + pltpu.roll(tn, HEAD_PAD - half, 1) * sa + pltpu.roll(tn, half, 1) * sb
            oref[:, sl] = (tr * sc).astype(BF)


def _in_proj(x2d, L, lw, rope, tm):
    T = x2d.shape[0]
    nt = T // tm
    r8 = tm // 8
    full = lambda a: pl.BlockSpec(a.shape, lambda i: (0,) * a.ndim)
    tps = L // tm
    weights = [lw['mix_norm'], lw['w_a'], lw['w_hy'], lw['cq_norm'], lw['ckv_norm'],
               lw['w_uq'], lw['w_uk'], lw['w_uv'], lw['p_kr'], lw['gq'], lw['gk']]
    in_specs = [
        pl.BlockSpec((tm, D_MODEL), lambda i: (i, 0)),
        pl.BlockSpec((8, D_MODEL), lambda i: (jnp.maximum(i * r8 - 1, 0), 0)),
        pl.BlockSpec((8, D_MODEL), lambda i: (jnp.minimum((i + 1) * r8, T // 8 - 1), 0)),
    ] + [full(w) for w in weights] + [
        pl.BlockSpec((tm, LANES), lambda i: (i % tps, 0)),
        pl.BlockSpec((tm, LANES), lambda i: (i % tps, 0)),
        pl.BlockSpec((tm, LANES), lambda i: (i % tps, 0)),
        full(lw['conv_w']), full(lw['conv_b']),
    ]
    out_shape = [jax.ShapeDtypeStruct((T, PAD_W), BF)] * 3 + [jax.ShapeDtypeStruct((T, HY_W), BF)] * 3
    out_specs = [pl.BlockSpec((tm, PAD_W), lambda i: (i, 0))] * 3 + [pl.BlockSpec((tm, HY_W), lambda i: (i, 0))] * 3
    return pl.pallas_call(
        functools.partial(_in_proj_kernel, tiles_per_seq=tps),
        out_shape=out_shape, grid=(nt,), in_specs=in_specs, out_specs=out_specs,
        compiler_params=_cparams(("parallel",)), name="in_proj",
    )(x2d, x2d, x2d, *weights, *rope, lw['conv_w'], lw['conv_b'])


def _attn_kernel(q_ref, k_ref, v_ref, o_ref, *, heads):
    lane = lax.broadcasted_iota(jnp.int32, (1, HEAD_PAD), 1)
    for h in range(heads):
        sl = slice(HEAD_PAD * h, HEAD_PAD * (h + 1))
        s = lax.dot_general(q_ref[:, sl], k_ref[:, sl], (((1,), (1,)), ((), ())), preferred_element_type=F32)
        m = jnp.max(s, axis=-1, keepdims=True)
        p = jnp.exp2(s - m).astype(BF)
        a = _dot(p, v_ref[:, sl])
        o = jnp.where(lane < V_DIM, a / a[:, V_DIM:V_DIM + 1], 0.0)
        o_ref[:, sl] = o.astype(o_ref.dtype)


def _attn(q, k, v, B, L, tq, heads):
    nq = L // tq
    w = heads * HEAD_PAD
    return pl.pallas_call(
        functools.partial(_attn_kernel, heads=heads),
        out_shape=jax.ShapeDtypeStruct(q.shape, BF),
        grid=(B, N_HEADS // heads, nq),
        in_specs=[
            pl.BlockSpec((tq, w), lambda b, h, qi: (b * nq + qi, h)),
            pl.BlockSpec((L, w), lambda b, h, qi: (b, h)),
            pl.BlockSpec((L, w), lambda b, h, qi: (b, h)),
        ],
        out_specs=pl.BlockSpec((tq, w), lambda b, h, qi: (b * nq + qi, h)),
        compiler_params=_cparams(("parallel", "parallel", "parallel")), name="attn",
    )(q, k, v)


def _filter_kernel(z_ref, w1_ref, b1_ref, fr_ref, w2_ref, b2_ref, w3_ref, dec_ref, o_ref):
    z = z_ref[...]
    fr = fr_ref[...]
    h = jnp.sin(fr * (jnp.dot(z, w1_ref[...], precision=HI, preferred_element_type=F32) + b1_ref[...]))
    h = jnp.sin(fr * (jnp.dot(h, w2_ref[...], precision=HI, preferred_element_type=F32) + b2_ref[...]))
    t = z[:, 0:1]
    o = jnp.dot(h, w3_ref[...], precision=HI, preferred_element_type=F32)
    o_ref[...] = o * jnp.exp(-t * jnp.abs(dec_ref[...]))


def _filters(zfeat, lw, tl):
    L = zfeat.shape[0]
    ch = lw['filt_w3'].shape[1]
    ws = [lw['filt_w1'], lw['filt_b1'], lw['filt_freq'], lw['filt_w2'], lw['filt_b2'], lw['filt_w3'],
          lw['filt_decay']]
    full = lambda a: pl.BlockSpec(a.shape, lambda i: (0,) * a.ndim)
    return pl.pallas_call(
        _filter_kernel,
        out_shape=jax.ShapeDtypeStruct((L, ch), F32),
        grid=(L // tl,),
        in_specs=[pl.BlockSpec((tl, LANES), lambda i: (i, 0))] + [full(w) for w in ws],
        out_specs=pl.BlockSpec((tl, ch), lambda i: (i, 0)),
        compiler_params=_cparams(("parallel",)), name="filt_mlp",
    )(zfeat, *ws)


def _dft_rows_kernel(x_ref, m_ref, o_ref):
    n1 = o_ref.shape[1]
    r = _dot(m_ref[...], x_ref[...].astype(BF))
    o_ref[0] = r[:n1].astype(o_ref.dtype)
    o_ref[1] = r[n1:].astype(o_ref.dtype)


def _dft_rows(xv, m, cb):
    P, n1, cols = xv.shape
    return pl.pallas_call(
        _dft_rows_kernel,
        out_shape=jax.ShapeDtypeStruct((P, 2, n1, cols), BF),
        grid=(P, cols // cb),
        in_specs=[pl.BlockSpec((None, n1, cb), lambda p, j: (p, 0, j)),
                  pl.BlockSpec(m.shape, lambda p, j: (0, 0))],
        out_specs=pl.BlockSpec((None, 2, n1, cb), lambda p, j: (p, 0, 0, j)),
        compiler_params=_cparams(("parallel", "parallel")), name="dft_rows",
    )(xv, m)


def _lane_tile(t, width):
    reps = width // t.shape[-1]
    return t if reps == 1 else jnp.concatenate([t] * reps, axis=-1)


def _filt_spec_kernel(a_ref, twr_ref, twi_ref, gf_ref, kr_ref, ki_ref, *, scale):
    kb, n2, c = kr_ref.shape
    gf = gf_ref[...]
    for j in range(kb):
        twr = _lane_tile(twr_ref[j], c)
        twi = _lane_tile(twi_ref[j], c)
        ar = a_ref[0, j].astype(F32)
        ai = a_ref[1, j].astype(F32)
        br = ar * twr - ai * twi
        bi = ar * twi + ai * twr
        x = _dot(gf, jnp.concatenate([br, bi], axis=0).astype(BF))
        kr_ref[j] = x[:n2] * scale
        ki_ref[j] = x[n2:] * scale


def _filt_spec(a5, twr, twi, gf, kb, scale):
    _, _, n1, n2, c = a5.shape
    return pl.pallas_call(
        functools.partial(_filt_spec_kernel, scale=scale),
        out_shape=[jax.ShapeDtypeStruct((n1, n2, c), F32)] * 2,
        grid=(n1 // kb,),
        in_specs=[pl.BlockSpec((None, 2, kb, n2, c), lambda i: (0, 0, i, 0, 0)),
                  pl.BlockSpec((kb, n2, LANES), lambda i: (i, 0, 0)),
                  pl.BlockSpec((kb, n2, LANES), lambda i: (i, 0, 0)),
                  pl.BlockSpec(gf.shape, lambda i: (0, 0))],
        out_specs=[pl.BlockSpec((kb, n2, c), lambda i: (i, 0, 0))] * 2,
        compiler_params=_cparams(("parallel",)), name="filt_spec",
    )(a5, twr, twi, gf)


def _spec_kernel(a_ref, twr_ref, twi_ref, kr_ref, ki_ref, gf_ref, gi_ref, o_ref):
    kb, n2, c = kr_ref.shape
    gf = gf_ref[...]
    gi = gi_ref[...]
    for j in range(kb):
        twr = _lane_tile(twr_ref[j], c)
        twi = _lane_tile(twi_ref[j], c)
        ar = a_ref[0, j].astype(F32)
        ai = a_ref[1, j].astype(F32)
        br = ar * twr - ai * twi
        bi = ar * twi + ai * twr
        x = _dot(gf, jnp.concatenate([br, bi], axis=0).astype(BF))
        xr, xi = x[:n2], x[n2:]
        kr = kr_ref[j]
        ki = ki_ref[j]
        zr = xr * kr - xi * ki
        zi = xr * ki + xi * kr
        y = _dot(gi, jnp.concatenate([zr, zi], axis=0).astype(BF))
        yr, yi = y[:n2], y[n2:]
        o_ref[0, j] = (yr * twr + yi * twi).astype(o_ref.dtype)
        o_ref[1, j] = (yi * twr - yr * twi).astype(o_ref.dtype)


def _spec(a5, twr, twi, kr, ki, gf, gi, kb):
    P, _, n1, n2, c = a5.shape
    return pl.pallas_call(
        _spec_kernel,
        out_shape=jax.ShapeDtypeStruct(a5.shape, BF),
        grid=(n1 // kb, P),
        in_specs=[pl.BlockSpec((None, 2, kb, n2, c), lambda i, p: (p, 0, i, 0, 0)),
                  pl.BlockSpec((kb, n2, LANES), lambda i, p: (i, 0, 0)),
                  pl.BlockSpec((kb, n2, LANES), lambda i, p: (i, 0, 0)),
                  pl.BlockSpec((kb, n2, c), lambda i, p: (i, 0, 0)),
                  pl.BlockSpec((kb, n2, c), lambda i, p: (i, 0, 0)),
                  pl.BlockSpec(gf.shape, lambda i, p: (0, 0)),
                  pl.BlockSpec(gi.shape, lambda i, p: (0, 0))],
        out_specs=pl.BlockSpec((None, 2, kb, n2, c), lambda i, p: (p, 0, i, 0, 0)),
        compiler_params=_cparams(("parallel", "parallel")), name="spec",
    )(a5, twr, twi, kr, ki, gf, gi)


def _idft_gate_kernel(b_ref, m_ref, u_ref, g_ref, d_ref, o_ref):
    _, n1, cb = b_ref.shape
    y = _dot(m_ref[...], b_ref[...].reshape(2 * n1, cb))
    u = u_ref[...].astype(F32)
    o_ref[...] = (g_ref[...].astype(F32) * (y + d_ref[...] * u)).astype(o_ref.dtype)


def _idft_gate(b4, m, uv, gv, d_t, cb):
    P, _, n1, cols = b4.shape
    return pl.pallas_call(
        _idft_gate_kernel,
        out_shape=jax.ShapeDtypeStruct((P, n1, cols), BF),
        grid=(P, cols // cb),
        in_specs=[pl.BlockSpec((None, 2, n1, cb), lambda p, j: (p, 0, 0, j)),
                  pl.BlockSpec(m.shape, lambda p, j: (0, 0)),
                  pl.BlockSpec((None, n1, cb), lambda p, j: (p, 0, j)),
                  pl.BlockSpec((None, n1, cb), lambda p, j: (p, 0, j)),
                  pl.BlockSpec((1, cb), lambda p, j: (0, 0))],
        out_specs=pl.BlockSpec((None, n1, cb), lambda p, j: (p, 0, j)),
        compiler_params=_cparams(("parallel", "parallel")), name="idft_gate",
    )(b4, m, uv, gv, d_t)


def _fft_plan(L):
    n = 2 * L
    n1 = 1 << ((n.bit_length() - 1 + 1) // 2)
    n2 = n // n1
    i1 = jnp.arange(n1, dtype=jnp.int32)
    i2 = jnp.arange(n2, dtype=jnp.int32)
    a1 = (2.0 * math.pi / n1) * ((i1[:, None] * i1[None, :]) % n1).astype(F32)
    c1, s1 = jnp.cos(a1), jnp.sin(a1)
    hn = n1 // 2
    m_sig = jnp.concatenate([jnp.concatenate([c1[:, :hn], s1[:, :hn]], 1),
                             jnp.concatenate([-s1[:, :hn], c1[:, :hn]], 1)], 0).astype(BF)
    m_filt = jnp.concatenate([c1, -s1], 0).astype(BF)
    m_inv = jnp.concatenate([jnp.concatenate([c1[:hn], -s1[:hn]], 1),
                             jnp.concatenate([s1[:hn], c1[:hn]], 1)], 0).astype(BF)
    a2 = (2.0 * math.pi / n2) * ((i2[:, None] * i2[None, :]) % n2).astype(F32)
    c2, s2 = jnp.cos(a2), jnp.sin(a2)
    gf = jnp.concatenate([jnp.concatenate([c2, s2], 1), jnp.concatenate([-s2, c2], 1)], 0).astype(BF)
    gi = jnp.concatenate([jnp.concatenate([c2, -s2], 1), jnp.concatenate([s2, c2], 1)], 0).astype(BF)
    at = (2.0 * math.pi / n) * (i1[:, None] * i2[None, :]).astype(F32)
    twr = jnp.broadcast_to(jnp.cos(at)[:, :, None], (n1, n2, LANES))
    twi = jnp.broadcast_to(-jnp.sin(at)[:, :, None], (n1, n2, LANES))
    return dict(n=n, n1=n1, n2=n2, m_sig=m_sig, m_filt=m_filt, m_inv=m_inv, gf=gf, gi=gi, twr=twr, twi=twi)


def _pick(total, want):
    b = min(total, want)
    while total % b:
        b //= 2
    return b


def _filter_spectra(filt, plan, L):
    n1, n2 = plan['n1'], plan['n2']
    f4 = filt.reshape(L, 2, 2, HY_W)
    out = []
    for o in range(2):
        kern = jnp.concatenate([f4[:, o, 0], jnp.zeros((1, HY_W), F32), f4[:0:-1, o, 1]], axis=0)
        cols = n2 * HY_W
        a = _dft_rows(kern.reshape(1, n1, cols), plan['m_filt'], _pick(cols, 4096))
        a5 = a.reshape(1, 2, n1, n2, HY_W)
        out.append(_filt_spec(a5, plan['twr'], plan['twi'], plan['gf'], _pick(n1, 8), 1.0 / plan['n']))
    return out


def _long_conv_gate(u, gate, d, kspec, plan, B, L):
    n1, n2 = plan['n1'], plan['n2']
    P = B // 2
    cols = n2 * HY_W
    cb = _pick(cols, 4096)
    uv = u.reshape(P, n1, cols)
    gv = gate.reshape(P, n1, cols)
    a = _dft_rows(uv, plan['m_sig'], cb)
    b = _spec(a.reshape(P, 2, n1, n2, HY_W), plan['twr'], plan['twi'], kspec[0], kspec[1],
              plan['gf'], plan['gi'], _pick(n1, 8))
    d_t = jnp.tile(d.reshape(1, HY_W).astype(F32), (1, cb // HY_W))
    z = _idft_gate(b.reshape(P, 2, n1, cols), plan['m_inv'], uv, gv, d_t, cb)
    return z.reshape(B * L, HY_W)


def _out_proj_kernel(x_ref, a_ref, hy_ref, ga_ref, gh_ref, wa_ref, wh_ref, o_ref):
    a = a_ref[...].astype(F32)
    hy = hy_ref[...].astype(F32)
    ra = lax.rsqrt(jnp.sum(a * a, axis=-1, keepdims=True) * (1.0 / ATTN_W) + EPS)
    rh = lax.rsqrt(jnp.sum(hy * hy, axis=-1, keepdims=True) * (1.0 / HY_W) + EPS)
    an = (a * ra * ga_ref[...]).astype(BF)
    hn = (hy * rh * gh_ref[...]).astype(BF)
    o_ref[...] = x_ref[...] + _dot(an, wa_ref[...]) + _dot(hn, wh_ref[...])


def _out_proj(x2d, attn, hy, lw, tm):
    T = x2d.shape[0]
    full = lambda a: pl.BlockSpec(a.shape, lambda i: (0,) * a.ndim)
    ws = [lw['ga'], lw['gh'], lw['w_out_a'], lw['w_out_h']]
    return pl.pallas_call(
        _out_proj_kernel,
        out_shape=jax.ShapeDtypeStruct((T, D_MODEL), F32),
        grid=(T // tm,),
        in_specs=[pl.BlockSpec((tm, D_MODEL), lambda i: (i, 0)),
                  pl.BlockSpec((tm, PAD_W), lambda i: (i, 0)),
                  pl.BlockSpec((tm, HY_W), lambda i: (i, 0))] + [full(w) for w in ws],
        out_specs=pl.BlockSpec((tm, D_MODEL), lambda i: (i, 0)),
        compiler_params=_cparams(("parallel",)), name="out_proj",
    )(x2d, attn, hy, *ws)


def _silu_mul(g, u):
    return g * (1.0 / (1.0 + jnp.exp(-g))) * u


def _ffn_kernel(x_ref, g_ref, wg_ref, wu_ref, wd_ref, o_ref, hn_sc, acc_sc):
    j = pl.program_id(1)

    @pl.when(j == 0)
    def _():
        hn_sc[...] = _rms(x_ref[...], g_ref[...]).astype(BF)
        acc_sc[...] = jnp.zeros(acc_sc.shape, F32)

    hn = hn_sc[...]
    a = _silu_mul(_dot(hn, wg_ref[...]), _dot(hn, wu_ref[...])).astype(BF)
    acc_sc[...] += _dot(a, wd_ref[...])

    @pl.when(j == pl.num_programs(1) - 1)
    def _():
        o_ref[...] = x_ref[...] + acc_sc[...]


def _ffn(x2d, g, wg, wu, wd, tm, tf):
    T = x2d.shape[0]
    ff = wg.shape[1]
    return pl.pallas_call(
        _ffn_kernel,
        out_shape=jax.ShapeDtypeStruct((T, D_MODEL), F32),
        grid=(T // tm, ff // tf),
        in_specs=[pl.BlockSpec((tm, D_MODEL), lambda i, j: (i, 0)),
                  pl.BlockSpec((1, D_MODEL), lambda i, j: (0, 0)),
                  pl.BlockSpec((D_MODEL, tf), lambda i, j: (0, j)),
                  pl.BlockSpec((D_MODEL, tf), lambda i, j: (0, j)),
                  pl.BlockSpec((tf, D_MODEL), lambda i, j: (j, 0))],
        out_specs=pl.BlockSpec((tm, D_MODEL), lambda i, j: (i, 0)),
        scratch_shapes=[pltpu.VMEM((tm, D_MODEL), BF), pltpu.VMEM((tm, D_MODEL), F32)],
        compiler_params=_cparams(("parallel", "arbitrary")), name="ffn",
    )(x2d, g, wg, wu, wd)


def _moe_kernel(x_ref, g_ref, rwt_ref, wg_ref, wu_ref, wd_ref, o_ref, hn_sc, sel_sc, pos_sc, gate_sc, acc_sc,
                *, cap):
    e = pl.program_id(1)
    j = pl.program_id(2)
    tm = x_ref.shape[0]

    @pl.when((e == 0) & (j == 0))
    def _():
        hn = _rms(x_ref[...], g_ref[...])
        hn_sc[...] = hn.astype(BF)
        lg = lax.dot_general(rwt_ref[...], hn, (((1,), (1,)), ((), ())), precision=HI, preferred_element_type=F32)
        row = lax.broadcasted_iota(jnp.int32, lg.shape, 0)
        m1 = jnp.max(lg, axis=0, keepdims=True)
        i1 = jnp.min(jnp.where(lg == m1, row, N_EXPERTS), axis=0, keepdims=True)
        lg2 = jnp.where(row == i1, -jnp.inf, lg)
        m2 = jnp.max(lg2, axis=0, keepdims=True)
        i2 = jnp.min(jnp.where(lg2 == m2, row, N_EXPERTS), axis=0, keepdims=True)
        e2 = jnp.exp(m2 - m1)
        g1 = 1.0 / (1.0 + e2)
        gate_sc[...] = jnp.where(row == i1, g1, 0.0) + jnp.where(row == i2, e2 * g1, 0.0)
        sel = jnp.where((row == i1) | (row == i2), 1.0, 0.0)
        sel_sc[...] = sel
        r = lax.broadcasted_iota(jnp.int32, (tm, tm), 0)
        c = lax.broadcasted_iota(jnp.int32, (tm, tm), 1)
        pos_sc[...] = _dot(sel.astype(BF), jnp.where(r < c, 1.0, 0.0).astype(BF))
        acc_sc[...] = jnp.zeros(acc_sc.shape, F32)

    sel_row = sel_sc[pl.ds(e, 1), :]
    pos_row = pos_sc[pl.ds(e, 1), :]
    gate_row = gate_sc[pl.ds(e, 1), :]
    count = jnp.sum(sel_row).astype(jnp.int32)
    slot = lax.broadcasted_iota(jnp.int32, (cap, tm), 0).astype(F32)

    def chunk(c, carry):
        base = (c * cap).astype(F32)
        onehot = jnp.where((pos_row - base == slot) & (sel_row > 0.0), 1.0, 0.0)
        gate = jnp.sum(onehot * gate_row, axis=1, keepdims=True)
        onehot = onehot.astype(BF)
        xg = _dot(onehot, hn_sc[...]).astype(BF)
        h = _silu_mul(_dot(xg, wg_ref[...]), _dot(xg, wu_ref[...])).astype(BF)
        y = (_dot(h, wd_ref[...]) * gate).astype(BF)
        acc_sc[...] += lax.dot_general(onehot, y, (((0,), (0,)), ((), ())), preferred_element_type=F32)
        return carry

    lax.fori_loop(0, (count + cap - 1) // cap, chunk, 0)

    @pl.when((e == pl.num_programs(1) - 1) & (j == pl.num_programs(2) - 1))
    def _():
        o_ref[...] = x_ref[...] + acc_sc[...]


def _moe(x2d, g, rwt, wg, wu, wd, tm, tf, cap):
    T = x2d.shape[0]
    ne, _, ff = wg.shape
    return pl.pallas_call(
        functools.partial(_moe_kernel, cap=cap),
        out_shape=jax.ShapeDtypeStruct((T, D_MODEL), F32),
        grid=(T // tm, ne, ff // tf),
        in_specs=[pl.BlockSpec((tm, D_MODEL), lambda i, e, j: (i, 0)),
                  pl.BlockSpec((1, D_MODEL), lambda i, e, j: (0, 0)),
                  pl.BlockSpec((ne, D_MODEL), lambda i, e, j: (0, 0)),
                  pl.BlockSpec((None, D_MODEL, tf), lambda i, e, j: (e, 0, j)),
                  pl.BlockSpec((None, D_MODEL, tf), lambda i, e, j: (e, 0, j)),
                  pl.BlockSpec((None, tf, D_MODEL), lambda i, e, j: (e, j, 0))],
        out_specs=pl.BlockSpec((tm, D_MODEL), lambda i, e, j: (i, 0)),
        scratch_shapes=[pltpu.VMEM((tm, D_MODEL), BF), pltpu.VMEM((ne, tm), F32), pltpu.VMEM((ne, tm), F32),
                        pltpu.VMEM((ne, tm), F32), pltpu.VMEM((tm, D_MODEL), F32)],
        compiler_params=_cparams(("parallel", "arbitrary", "arbitrary")), name="moe",
    )(x2d, g, rwt, wg, wu, wd)


def _pad_heads(w, d):
    lead = w.shape[:-1]
    w = w.reshape(lead + (N_HEADS, d))
    w = jnp.pad(w, [(0, 0)] * len(lead) + [(0, 0), (0, HEAD_PAD - d)])
    return w.reshape(lead + (PAD_W,))


def _layer_weights(p, layer):
    (mix_norm, w_in, cq_norm, ckv_norm, w_uq, w_ukv, q_norm, k_norm, conv_w, conv_b,
     filt_w1, filt_b1, filt_freq, filt_w2, filt_b2, filt_w3, filt_decay, hyena_d,
     out_norm, w_out, ffn_norm) = [a[layer] for a in p]
    na = Q_LORA + KV_LORA + QK_ROPE
    ukv = w_ukv.reshape(KV_LORA, N_HEADS, QK_NOPE + V_DIM)
    eye = jnp.eye(QK_ROPE, dtype=F32)
    p_kr = jnp.pad(eye, ((0, 0), (QK_NOPE, HEAD_PAD - QK_DIM)))
    row = lambda a: a.reshape(1, -1).astype(F32)
    w_out_a = w_out[:ATTN_W].reshape(N_HEADS, V_DIM, D_MODEL)
    w_out_a = jnp.pad(w_out_a, ((0, 0), (0, HEAD_PAD - V_DIM), (0, 0))).reshape(PAD_W, D_MODEL)
    return dict(
        mix_norm=row(mix_norm), w_a=w_in[:, :na].astype(BF), w_hy=w_in[:, na:].astype(BF),
        cq_norm=row(cq_norm), ckv_norm=row(ckv_norm),
        w_uq=_pad_heads(w_uq, QK_DIM).astype(BF),
        w_uk=_pad_heads(ukv[:, :, :QK_NOPE].reshape(KV_LORA, -1), QK_NOPE).astype(BF),
        w_uv=_pad_heads(ukv[:, :, QK_NOPE:].reshape(KV_LORA, -1), V_DIM).astype(BF),
        p_kr=jnp.tile(p_kr, (1, N_HEADS)).astype(BF),
        gq=jnp.tile(jnp.pad(q_norm, (0, HEAD_PAD - QK_DIM)), N_HEADS).reshape(1, PAD_W),
        gk=jnp.tile(jnp.pad(k_norm, (0, HEAD_PAD - QK_DIM)), N_HEADS).reshape(1, PAD_W),
        conv_w=conv_w.astype(F32), conv_b=row(conv_b),
        filt_w1=jnp.pad(filt_w1, ((0, LANES - POS_EMB), (0, 0))), filt_b1=row(filt_b1),
        filt_freq=row(filt_freq), filt_w2=filt_w2, filt_b2=row(filt_b2), filt_w3=filt_w3,
        filt_decay=row(filt_decay), hyena_d=hyena_d,
        ga=_pad_heads(out_norm[:ATTN_W], V_DIM).reshape(1, PAD_W), gh=row(out_norm[ATTN_W:]),
        w_out_a=w_out_a.astype(BF), w_out_h=w_out[ATTN_W:].astype(BF),
        ffn_norm=row(ffn_norm),
    )


def _position_tables(L):
    pos = jnp.arange(L, dtype=F32)
    half = QK_ROPE // 2
    inv = ROPE_THETA ** (-jnp.arange(0, QK_ROPE, 2, dtype=F32) / QK_ROPE)
    ang = pos[:, None] * inv[None, :]
    cos, sin = jnp.cos(ang), jnp.sin(ang)
    zeros = lambda w: jnp.zeros((L, w), F32)
    cos_t = jnp.concatenate([jnp.ones((L, QK_NOPE), F32), cos, cos, zeros(HEAD_PAD - QK_DIM)], 1)
    sin_a = jnp.concatenate([zeros(QK_NOPE), -sin, zeros(HEAD_PAD - QK_NOPE - half)], 1)
    sin_b = jnp.concatenate([zeros(QK_NOPE + half), sin, zeros(HEAD_PAD - QK_DIM)], 1)
    t = jnp.linspace(0.0, 1.0, L, dtype=F32)[:, None]
    w = 2.0 * math.pi * jnp.arange(L, dtype=F32) / L
    f = jnp.linspace(1e-4, N_BANDS - 1, N_BANDS, dtype=F32)
    a = w[:, None] * f[None, :]
    z = jnp.concatenate([t, jnp.cos(a), -jnp.sin(a), jnp.zeros((L, LANES - POS_EMB), F32)], axis=-1)
    return (cos_t, sin_a, sin_b), z


def _moe_capacity(tm):
    return max(16, (tm * 5 // 16) // 16 * 16)


def _trunk(x, layers, mixers, plan, rope, zfeat):
    B, L, _ = x.shape
    T = B * L
    x2d = x.reshape(T, D_MODEL)
    tm = _pick(L, 512)
    for lw, mixer in zip(layers, mixers):
        q, k, v, hv, hx1, hx2 = _in_proj(x2d, L, lw, rope, tm)
        attn = _attn(q, k, v, B, L, _pick(L, ATTN_TQ), 1 if L * HEAD_PAD * 2 > ATTN_KV_BYTES else 2)
        filt = _filters(zfeat, lw, _pick(L, 512))
        ks = _filter_spectra(filt, plan, L)
        z1 = _long_conv_gate(hv, hx1, lw['hyena_d'][0], ks[0], plan, B, L)
        hy = _long_conv_gate(z1, hx2, lw['hyena_d'][1], ks[1], plan, B, L)
        x2d = _out_proj(x2d, attn, hy, lw, tm)
        tf = 1408
        if mixer[0] == 'dense':
            _, wg, wu, wd = mixer
            x2d = _ffn(x2d, lw['ffn_norm'], wg, wu, wd, _pick(T, 1024), tf)
        else:
            _, rwt, wg, wu, wd = mixer
            tme = _pick(T, 1024)
            x2d = _moe(x2d, lw['ffn_norm'], rwt, wg, wu, wd, tme, tf, _moe_capacity(tme))
    return x2d.reshape(B, L, D_MODEL)


def kernel(x_prompt, x_sample, mix_norm, w_in, cq_norm, ckv_norm, w_uq, w_ukv, q_norm, k_norm, conv_w, conv_b, filt_w1, filt_b1, filt_freq, filt_w2, filt_b2, filt_w3, filt_decay, hyena_d, out_norm, w_out, ffn_norm, dense_wg, dense_wu, dense_wd, router_w, moe_wg, moe_wu, moe_wd):
    per_layer = (mix_norm, w_in, cq_norm, ckv_norm, w_uq, w_ukv, q_norm, k_norm, conv_w, conv_b,
                 filt_w1, filt_b1, filt_freq, filt_w2, filt_b2, filt_w3, filt_decay, hyena_d,
                 out_norm, w_out, ffn_norm)
    depth = mix_norm.shape[0]
    layers = [_layer_weights(per_layer, l) for l in range(depth)]
    mixers = []
    for l in range(depth):
        j = l // 2
        if l % 2 == 0:
            mixers.append(('dense', dense_wg[j].astype(BF), dense_wu[j].astype(BF), dense_wd[j].astype(BF)))
        else:
            mixers.append(('moe', router_w[j].T,moe_wg[j].astype(BF), moe_wu[j].astype(BF), moe_wd[j].astype(BF)))
    outs = []
    for x in (x_prompt, x_sample):
        L = x.shape[1]
        rope, zfeat = _position_tables(L)
        outs.append(_trunk(x, layers, mixers, _fft_plan(L), rope, zfeat))
    return tuple(outs)
```

```python
import functools
import math

import jax
import jax.numpy as jnp
from jax import lax
from jax.experimental import pallas as pl
from jax.experimental.pallas import tpu as pltpu

BF = jnp.bfloat16
F32 = jnp.float32

D_MODEL = 1024
N_HEADS = 8
QK_NOPE = 64
QK_ROPE = 32
QK_DIM = QK_NOPE + QK_ROPE
V_DIM = 64
Q_LORA = 256
KV_LORA = 128
ATTN_W = N_HEADS * V_DIM
HY_W = D_MODEL - ATTN_W
ROPE_THETA = 10000.0
N_BANDS = 16
POS_EMB = 1 + 2 * N_BANDS
FILT_HID = 64
N_EXPERTS = 8
EPS = 1e-6

LANES = 128
HEAD_PAD = LANES
PAD_W = N_HEADS * HEAD_PAD
VMEM_LIMIT = 56 * 1024 * 1024
HI = lax.Precision.HIGHEST
ATTN_TQ = 256
ATTN_KV_BYTES = 2 * 1024 * 1024
HY_CB = 16


def _cparams(sem):
    return pltpu.CompilerParams(dimension_semantics=sem, vmem_limit_bytes=VMEM_LIMIT)


def _rms(x, g):
    ms = jnp.mean(x * x, axis=-1, keepdims=True)
    return x * lax.rsqrt(ms + EPS) * g


def _dot(a, b):
    return jnp.dot(a, b, preferred_element_type=F32)


def _full(a):
    return pl.BlockSpec(a.shape, lambda *_: (0,) * a.ndim)


def _in_proj_kernel(x_ref, gmix_ref, wa_ref, cqn_ref, ckvn_ref, wuq_ref, wuk_ref, wuv_ref, pkr_ref,
                    gq_ref, gk_ref, cos_ref, sina_ref, sinb_ref, q_ref, k_ref, v_ref):
    h = _rms(x_ref[...], gmix_ref[...]).astype(BF)
    pa = _dot(h, wa_ref[...])
    c_q = pa[:, 0:Q_LORA]
    c_kv = pa[:, Q_LORA:Q_LORA + KV_LORA]
    k_r = pa[:, Q_LORA + KV_LORA:Q_LORA + KV_LORA + QK_ROPE]
    cqn = _rms(c_q, cqn_ref[...]).astype(BF)
    ckvn = _rms(c_kv, ckvn_ref[...]).astype(BF)
    q = _dot(cqn, wuq_ref[...])
    k = _dot(ckvn, wuk_ref[...]) + _dot(k_r.astype(BF), pkr_ref[...])
    lane = lax.broadcasted_iota(jnp.int32, (1, PAD_W), 1)
    ones_col = jnp.where(lane % HEAD_PAD == V_DIM, 1.0, 0.0)
    v_ref[...] = (_dot(ckvn, wuv_ref[...]) + ones_col).astype(BF)
    cos = cos_ref[...]
    sa = sina_ref[...]
    sb = sinb_ref[...]
    half = QK_ROPE // 2
    scale = QK_DIM ** -0.5 * math.log2(math.e)
    for hh in range(N_HEADS):
        sl = slice(HEAD_PAD * hh, HEAD_PAD * (hh + 1))
        for src, gref, oref, sc in ((q, gq_ref, q_ref, scale), (k, gk_ref, k_ref, 1.0)):
            t = src[:, sl]
            ms = jnp.sum(t * t, axis=-1, keepdims=True) * (1.0 / QK_DIM)
            tn = t * lax.rsqrt(ms + EPS) * gref[:, sl]
            tr = tn * cos + pltpu.roll(tn, HEAD_PAD - half, 1) * sa + pltpu.roll(tn, half, 1) * sb
            oref[:, sl] = (tr * sc).astype(BF)


def _in_proj(x2d, L, lw, rope, tm):
    T = x2d.shape[0]
    tps = L // tm
    weights = [lw['mix_norm'], lw['w_a'], lw['cq_norm'], lw['ckv_norm'],
               lw['w_uq'], lw['w_uk'], lw['w_uv'], lw['p_kr'], lw['gq'], lw['gk']]
    in_specs = ([pl.BlockSpec((tm, D_MODEL), lambda i: (i, 0))] + [_full(w) for w in weights]
                + [pl.BlockSpec((tm, LANES), lambda i: (i % tps, 0))] * 3)
    return pl.pallas_call(
        _in_proj_kernel,
        out_shape=[jax.ShapeDtypeStruct((T, PAD_W), BF)] * 3, grid=(T // tm,), in_specs=in_specs,
        out_specs=[pl.BlockSpec((tm, PAD_W), lambda i: (i, 0))] * 3,
        compiler_params=_cparams(("parallel",)), name="in_proj",
    )(x2d, *weights, *rope)


def _hy_proj_kernel(x_ref, gmix_ref, wt_ref, v_ref, x1_ref, x2_ref):
    h = _rms(x_ref[...], gmix_ref[...]).astype(BF)
    for g, oref in enumerate((v_ref, x1_ref, x2_ref)):
        ut = lax.dot_general(wt_ref[g * HY_W:(g + 1) * HY_W, :], h, (((1,), (1,)), ((), ())),
                             preferred_element_type=F32)
        for a in range(oref.shape[1]):
            oref[:, a, :] = ut[:, a * LANES:(a + 1) * LANES].astype(oref.dtype)


def _hy_proj(x2d, B, L, lw, tm):
    tps = L // tm
    rows = tm // LANES
    shape = jax.ShapeDtypeStruct((B, HY_W, L // LANES, LANES), BF)
    spec = pl.BlockSpec((None, HY_W, rows, LANES), lambda i: (i // tps, 0, i % tps, 0))
    return pl.pallas_call(
        _hy_proj_kernel, out_shape=[shape] * 3, grid=(B * tps,),
        in_specs=[pl.BlockSpec((tm, D_MODEL), lambda i: (i, 0)), _full(lw['mix_norm']), _full(lw['w_hy_t'])],
        out_specs=[spec] * 3,
        compiler_params=_cparams(("parallel",)), name="hy_proj",
    )(x2d, lw['mix_norm'], lw['w_hy_t'])


def _attn_kernel(q_ref, k_ref, v_ref, o_ref, *, heads):
    lane = lax.broadcasted_iota(jnp.int32, (1, HEAD_PAD), 1)
    for h in range(heads):
        sl = slice(HEAD_PAD * h, HEAD_PAD * (h + 1))
        s = lax.dot_general(q_ref[:, sl], k_ref[:, sl], (((1,), (1,)), ((), ())), preferred_element_type=F32)
        m = jnp.max(s, axis=-1, keepdims=True)
        p = jnp.exp2(s - m).astype(BF)
        a = _dot(p, v_ref[:, sl])
        o = jnp.where(lane < V_DIM, a / a[:, V_DIM:V_DIM + 1], 0.0)
        o_ref[:, sl] = o.astype(o_ref.dtype)


def _attn(q, k, v, B, L, tq, heads):
    nq = L // tq
    w = heads * HEAD_PAD
    return pl.pallas_call(
        functools.partial(_attn_kernel, heads=heads),
        out_shape=jax.ShapeDtypeStruct(q.shape, BF),
        grid=(B, N_HEADS // heads, nq),
        in_specs=[
            pl.BlockSpec((tq, w), lambda b, h, qi: (b * nq + qi, h)),
            pl.BlockSpec((L, w), lambda b, h, qi: (b, h)),
            pl.BlockSpec((L, w), lambda b, h, qi: (b, h)),
        ],
        out_specs=pl.BlockSpec((tq, w), lambda b, h, qi: (b * nq + qi, h)),
        compiler_params=_cparams(("parallel", "parallel", "parallel")), name="attn",
    )(q, k, v)


def _filter_kernel(z_ref, w1_ref, b1_ref, fr_ref, w2_ref, b2_ref, w3_ref, dec_ref, o_ref, *, seq_len):
    z = z_ref[...]
    fr = fr_ref[...]
    h = jnp.sin(fr * (jnp.dot(w1_ref[...], z, precision=HI, preferred_element_type=F32) + b1_ref[...]))
    h = jnp.sin(fr * (jnp.dot(w2_ref[...], h, precision=HI, preferred_element_type=F32) + b2_ref[...]))
    t = z[0:1, :]
    o = jnp.dot(w3_ref[...], h, precision=HI, preferred_element_type=F32) * jnp.exp(-t * jnp.abs(dec_ref[...]))
    n = pl.program_id(0) * z.shape[1] + lax.broadcasted_iota(jnp.int32, (1, z.shape[1]), 1)
    o = jnp.where(n == seq_len, 0.0, o)
    for a in range(o_ref.shape[1]):
        o_ref[:, a, :] = o[:, a * LANES:(a + 1) * LANES]


def _filter_taps(zt, lw, L, tl):
    n = 2 * L
    ch = lw['filt_w3_t'].shape[1]
    half_tiles = L // tl
    sel = lambda i: (i // half_tiles, 0, 0)
    ws = [lw['filt_w1_t'], lw['filt_b1'], lw['filt_freq'], lw['filt_w2_t'], lw['filt_b2']]
    return pl.pallas_call(
        functools.partial(_filter_kernel, seq_len=L),
        out_shape=jax.ShapeDtypeStruct((ch, n // LANES, LANES), F32),
        grid=(n // tl,),
        in_specs=[pl.BlockSpec((LANES, tl), lambda i: (0, i))] + [_full(w) for w in ws]
        + [pl.BlockSpec((None, ch, FILT_HID), sel), pl.BlockSpec((None, ch, 1), sel)],
        out_specs=pl.BlockSpec((ch, tl // LANES, LANES), lambda i: (0, i, 0)),
        compiler_params=_cparams(("parallel",)), name="filt_taps",
    )(zt, *ws, lw['filt_w3_t'], lw['filt_decay_t'])


def _lanes_of(parts):
    return parts[0] if len(parts) == 1 else jnp.concatenate(parts, axis=1)


def _hyena_kernel(v_ref, x1_ref, x2_ref, kt_ref, cw_ref, cb_ref, d_ref, msig_ref, mfilt_ref, minv_ref,
                  g2f_ref, g2i_ref, twr_ref, twi_ref, o_ref, kf_sc, *, scale):
    _, cb, nh, _ = v_ref.shape
    n1 = 2 * nh
    twr = twr_ref[...]
    twi = twi_ref[...]

    def fwd(s_all, m):
        a = _dot(m, s_all)
        rows = []
        for c in range(cb):
            ar = a[:n1, c * LANES:(c + 1) * LANES]
            ai = a[n1:, c * LANES:(c + 1) * LANES]
            rows.append(jnp.concatenate([ar * twr - ai * twi, ar * twi + ai * twr], axis=1).astype(BF))
        return _dot(jnp.concatenate(rows, axis=0), g2f_ref[...])

    def inv(z):
        b = _dot(z.astype(BF), g2i_ref[...])
        cols = []
        for c in range(cb):
            br = b[c * n1:(c + 1) * n1, :LANES]
            bi = b[c * n1:(c + 1) * n1, LANES:]
            cols.append(jnp.concatenate([br * twr + bi * twi, bi * twr - br * twi], axis=0).astype(BF))
        return _dot(minv_ref[...], _lanes_of(cols))

    def cmul(x, kf):
        xr, xi = x[:, :LANES], x[:, LANES:]
        kr, ki = kf[:, :LANES], kf[:, LANES:]
        return jnp.concatenate([xr * kr - xi * ki, xr * ki + xi * kr], axis=1)

    @pl.when(pl.program_id(1) == 0)
    def _():
        for o in range(2):
            taps = _lanes_of([kt_ref[o, c] for c in range(cb)]).astype(BF)
            kf_sc[o] = fwd(taps, mfilt_ref[...]) * scale

    row = lax.broadcasted_iota(jnp.int32, (cb * nh, LANES), 0) % nh
    lane = lax.broadcasted_iota(jnp.int32, (cb * nh, LANES), 1)

    def short_conv(ref, g):
        w = [jnp.broadcast_to(cw_ref[k, g], (cb, nh, LANES)).reshape(cb * nh, LANES) for k in range(3)]
        bias = jnp.broadcast_to(cb_ref[g], (cb, nh, LANES)).reshape(cb * nh, LANES)
        halves = []
        for r in range(2):
            u = ref[r].astype(F32).reshape(cb * nh, LANES)
            lp = pltpu.roll(u, 1, 1)
            prev = jnp.where(lane == 0, pltpu.roll(lp, 1, 0), lp)
            prev = jnp.where((lane == 0) & (row == 0), 0.0, prev)
            ln = pltpu.roll(u, LANES - 1, 1)
            nxt = jnp.where(lane == LANES - 1, pltpu.roll(ln, cb * nh - 1, 0), ln)
            nxt = jnp.where((lane == LANES - 1) & (row == nh - 1), 0.0, nxt)
            y = prev * w[0] + u * w[1] + nxt * w[2] + bias
            halves.append(_lanes_of([y[c * nh:(c + 1) * nh] for c in range(cb)]))
        return jnp.concatenate(halves, axis=0)

    def skip(o):
        return _lanes_of([d_ref[o, c] for c in range(cb)])

    v = short_conv(v_ref, 0)
    x1 = short_conv(x1_ref, 1)
    x2 = short_conv(x2_ref, 2)
    y = inv(cmul(fwd(v.astype(BF), msig_ref[...]), kf_sc[0]))
    z1 = (x1 * (y + skip(0) * v)).astype(BF)
    y = inv(cmul(fwd(z1, msig_ref[...]), kf_sc[1]))
    hy = x2 * (y + skip(1) * z1.astype(F32))
    for c in range(cb):
        o_ref[0, :, c, :] = hy[:nh, c * LANES:(c + 1) * LANES].astype(o_ref.dtype)
        o_ref[1, :, c, :] = hy[nh:, c * LANES:(c + 1) * LANES].astype(o_ref.dtype)


def _hyena(hv, hx1, hx2, taps, lw, plan, B, L):
    nh = L // LANES
    n1 = 2 * nh
    P = B // 2
    cb = HY_CB
    pair = lambda a: a.reshape(P, 2, HY_W, nh, LANES)
    sig = pl.BlockSpec((None, 2, cb, nh, LANES), lambda j, p: (p, 0, j, 0, 0))
    consts = [plan['m_sig'], plan['m_filt'], plan['m_inv'], plan['g2f'], plan['g2i'], plan['twr'], plan['twi']]
    out = pl.pallas_call(
        functools.partial(_hyena_kernel, scale=1.0 / (2 * L)),
        out_shape=jax.ShapeDtypeStruct((P, 2, nh, HY_W, LANES), BF),
        grid=(HY_W // cb, P),
        in_specs=[sig, sig, sig,
                  pl.BlockSpec((2, cb, n1, LANES), lambda j, p: (0, j, 0, 0)),
                  pl.BlockSpec((3, 3, cb, 1, LANES), lambda j, p: (0, 0, j, 0, 0)),
                  pl.BlockSpec((3, cb, 1, LANES), lambda j, p: (0, j, 0, 0)),
                  pl.BlockSpec((2, cb, 1, LANES), lambda j, p: (0, j, 0, 0))] + [_full(c) for c in consts],
        out_specs=pl.BlockSpec((None, 2, nh, cb, LANES), lambda j, p: (p, 0, 0, j, 0)),
        scratch_shapes=[pltpu.VMEM((2, cb * n1, 2 * LANES), F32)],
        compiler_params=_cparams(("parallel", "arbitrary")), name="hyena",
    )(pair(hv), pair(hx1), pair(hx2), taps.reshape(2, HY_W, n1, LANES), lw['conv_w_t'], lw['conv_b_t'],
      lw['hyena_d_t'], *consts)
    return out.reshape(B, nh, HY_W, LANES)


def _fft_plan(L):
    n = 2 * L
    n2 = LANES
    n1 = n // n2
    i1 = jnp.arange(n1, dtype=jnp.int32)
    i2 = jnp.arange(n2, dtype=jnp.int32)
    a1 = (2.0 * math.pi / n1) * ((i1[:, None] * i1[None, :]) % n1).astype(F32)
    c1, s1 = jnp.cos(a1), jnp.sin(a1)
    hn = n1 // 2
    m_sig = jnp.concatenate([jnp.concatenate([c1[:, :hn], s1[:, :hn]], 1),
                             jnp.concatenate([-s1[:, :hn], c1[:, :hn]], 1)], 0).astype(BF)
    m_filt = jnp.concatenate([c1, -s1], 0).astype(BF)
    m_inv = jnp.concatenate([jnp.concatenate([c1[:hn], -s1[:hn]], 1),
                             jnp.concatenate([s1[:hn], c1[:hn]], 1)], 0).astype(BF)
    a2 = (2.0 * math.pi / n2) * ((i2[:, None] * i2[None, :]) % n2).astype(F32)
    c2, s2 = jnp.cos(a2), jnp.sin(a2)
    g2f = jnp.concatenate([jnp.concatenate([c2, -s2], 1), jnp.concatenate([s2, c2], 1)], 0).astype(BF)
    g2i = jnp.concatenate([jnp.concatenate([c2, s2], 1), jnp.concatenate([-s2, c2], 1)], 0).astype(BF)
    at = (2.0 * math.pi / n) * (i1[:, None] * i2[None, :]).astype(F32)
    return dict(m_sig=m_sig, m_filt=m_filt, m_inv=m_inv, g2f=g2f, g2i=g2i, twr=jnp.cos(at), twi=-jnp.sin(at))


def _pick(total, want):
    b = min(total, want)
    while total % b:
        b //= 2
    return b


def _out_proj_kernel(x_ref, a_ref, hy_ref, ga_ref, gh_ref, wa_ref, wh_ref, o_ref):
    a = a_ref[...].astype(F32)
    ra = lax.rsqrt(jnp.sum(a * a, axis=-1, keepdims=True) * (1.0 / ATTN_W) + EPS)
    an = (a * ra * ga_ref[...]).astype(BF)
    acc = x_ref[...] + _dot(an, wa_ref[...])
    gh = gh_ref[...]
    wh = wh_ref[...]
    for s in range(hy_ref.shape[0]):
        hy = hy_ref[s].astype(F32)
        rh = lax.rsqrt(jnp.sum(hy * hy, axis=0, keepdims=True) * (1.0 / HY_W) + EPS)
        hn = (hy * rh * gh).astype(BF)
        part = lax.dot_general(hn, wh, (((0,), (0,)), ((), ())), preferred_element_type=F32)
        o_ref[s * LANES:(s + 1) * LANES, :] = acc[s * LANES:(s + 1) * LANES, :] + part


def _out_proj(x2d, attn, hy, lw, B, L, tm):
    T = x2d.shape[0]
    tps = L // tm
    ws = [lw['ga'], lw['gh_t'], lw['w_out_a'], lw['w_out_h']]
    return pl.pallas_call(
        _out_proj_kernel,
        out_shape=jax.ShapeDtypeStruct((T, D_MODEL), F32),
        grid=(T // tm,),
        in_specs=[pl.BlockSpec((tm, D_MODEL), lambda i: (i, 0)),
                  pl.BlockSpec((tm, PAD_W), lambda i: (i, 0)),
                  pl.BlockSpec((None, tm // LANES, HY_W, LANES), lambda i: (i // tps, i % tps, 0, 0))]
        + [_full(w) for w in ws],
        out_specs=pl.BlockSpec((tm, D_MODEL), lambda i: (i, 0)),
        compiler_params=_cparams(("parallel",)), name="out_proj",
    )(x2d, attn, hy, *ws)


def _silu_mul(g, u):
    return g * (1.0 / (1.0 + jnp.exp(-g))) * u


def _ffn_kernel(x_ref, g_ref, wg_ref, wu_ref, wd_ref, o_ref, hn_sc, acc_sc):
    j = pl.program_id(1)

    @pl.when(j == 0)
    def _():
        hn_sc[...] = _rms(x_ref[...], g_ref[...]).astype(BF)
        acc_sc[...] = jnp.zeros(acc_sc.shape, F32)

    hn = hn_sc[...]
    a = _silu_mul(_dot(hn, wg_ref[...]), _dot(hn, wu_ref[...])).astype(BF)
    acc_sc[...] += _dot(a, wd_ref[...])

    @pl.when(j == pl.num_programs(1) - 1)
    def _():
        o_ref[...] = x_ref[...] + acc_sc[...]


def _ffn(x2d, g, wg, wu, wd, tm, tf):
    T = x2d.shape[0]
    ff = wg.shape[1]
    return pl.pallas_call(
        _ffn_kernel,
        out_shape=jax.ShapeDtypeStruct((T, D_MODEL), F32),
        grid=(T // tm, ff // tf),
        in_specs=[pl.BlockSpec((tm, D_MODEL), lambda i, j: (i, 0)),
                  pl.BlockSpec((1, D_MODEL), lambda i, j: (0, 0)),
                  pl.BlockSpec((D_MODEL, tf), lambda i, j: (0, j)),
                  pl.BlockSpec((D_MODEL, tf), lambda i, j: (0, j)),
                  pl.BlockSpec((tf, D_MODEL), lambda i, j: (j, 0))],
        out_specs=pl.BlockSpec((tm, D_MODEL), lambda i, j: (i, 0)),
        scratch_shapes=[pltpu.VMEM((tm, D_MODEL), BF), pltpu.VMEM((tm, D_MODEL), F32)],
        compiler_params=_cparams(("parallel", "arbitrary")), name="ffn",
    )(x2d, g, wg, wu, wd)


def _moe_kernel(x_ref, g_ref, rwt_ref, wg_ref, wu_ref, wd_ref, o_ref, hn_sc, sel_sc, pos_sc, gate_sc, acc_sc,
                *, cap):
    e = pl.program_id(1)
    j = pl.program_id(2)
    tm = x_ref.shape[0]

    @pl.when((e == 0) & (j == 0))
    def _():
        hn = _rms(x_ref[...], g_ref[...])
        hn_sc[...] = hn.astype(BF)
        lg = lax.dot_general(rwt_ref[...], hn, (((1,), (1,)), ((), ())), precision=HI, preferred_element_type=F32)
        row = lax.broadcasted_iota(jnp.int32, lg.shape, 0)
        m1 = jnp.max(lg, axis=0, keepdims=True)
        i1 = jnp.min(jnp.where(lg == m1, row, N_EXPERTS), axis=0, keepdims=True)
        lg2 = jnp.where(row == i1, -jnp.inf, lg)
        m2 = jnp.max(lg2, axis=0, keepdims=True)
        i2 = jnp.min(jnp.where(lg2 == m2, row, N_EXPERTS), axis=0, keepdims=True)
        e2 = jnp.exp(m2 - m1)
        g1 = 1.0 / (1.0 + e2)
        gate_sc[...] = jnp.where(row == i1, g1, 0.0) + jnp.where(row == i2, e2 * g1, 0.0)
        sel = jnp.where((row == i1) | (row == i2), 1.0, 0.0)
        sel_sc[...] = sel
        r = lax.broadcasted_iota(jnp.int32, (tm, tm), 0)
        c = lax.broadcasted_iota(jnp.int32, (tm, tm), 1)
        pos_sc[...] = _dot(sel.astype(BF), jnp.where(r < c, 1.0, 0.0).astype(BF))
        acc_sc[...] = jnp.zeros(acc_sc.shape, F32)

    sel_row = sel_sc[pl.ds(e, 1), :]
    pos_row = pos_sc[pl.ds(e, 1), :]
    gate_row = gate_sc[pl.ds(e, 1), :]
    count = jnp.sum(sel_row).astype(jnp.int32)
    slot = lax.broadcasted_iota(jnp.int32, (cap, tm), 0).astype(F32)

    def chunk(c, carry):
        base = (c * cap).astype(F32)
        onehot = jnp.where((pos_row - base == slot) & (sel_row > 0.0), 1.0, 0.0)
        gate = jnp.sum(onehot * gate_row, axis=1, keepdims=True)
        onehot = onehot.astype(BF)
        xg = _dot(onehot, hn_sc[...]).astype(BF)
        h = _silu_mul(_dot(xg, wg_ref[...]), _dot(xg, wu_ref[...])).astype(BF)
        y = (_dot(h, wd_ref[...]) * gate).astype(BF)
        acc_sc[...] += lax.dot_general(onehot, y, (((0,), (0,)), ((), ())), preferred_element_type=F32)
        return carry

    lax.fori_loop(0, (count + cap - 1) // cap, chunk, 0)

    @pl.when((e == pl.num_programs(1) - 1) & (j == pl.num_programs(2) - 1))
    def _():
        o_ref[...] = x_ref[...] + acc_sc[...]


def _moe(x2d, g, rwt, wg, wu, wd, tm, tf, cap):
    T = x2d.shape[0]
    ne, _, ff = wg.shape
    return pl.pallas_call(
        functools.partial(_moe_kernel, cap=cap),
        out_shape=jax.ShapeDtypeStruct((T, D_MODEL), F32),
        grid=(T // tm, ne, ff // tf),
        in_specs=[pl.BlockSpec((tm, D_MODEL), lambda i, e, j: (i, 0)),
                  pl.BlockSpec((1, D_MODEL), lambda i, e, j: (0, 0)),
                  pl.BlockSpec((ne, D_MODEL), lambda i, e, j: (0, 0)),
                  pl.BlockSpec((None, D_MODEL, tf), lambda i, e, j: (e, 0, j)),
                  pl.BlockSpec((None, D_MODEL, tf), lambda i, e, j: (e, 0, j)),
                  pl.BlockSpec((None, tf, D_MODEL), lambda i, e, j: (e, j, 0))],
        out_specs=pl.BlockSpec((tm, D_MODEL), lambda i, e, j: (i, 0)),
        scratch_shapes=[pltpu.VMEM((tm, D_MODEL), BF), pltpu.VMEM((ne, tm), F32), pltpu.VMEM((ne, tm), F32),
                        pltpu.VMEM((ne, tm), F32), pltpu.VMEM((tm, D_MODEL), F32)],
        compiler_params=_cparams(("parallel", "arbitrary", "arbitrary")), name="moe",
    )(x2d, g, rwt, wg, wu, wd)


def _pad_heads(w, d):
    lead = w.shape[:-1]
    w = w.reshape(lead + (N_HEADS, d))
    w = jnp.pad(w, [(0, 0)] * len(lead) + [(0, 0), (0, HEAD_PAD - d)])
    return w.reshape(lead + (PAD_W,))


def _per_channel(a):
    return jnp.broadcast_to(a.astype(F32)[..., None, None], a.shape + (1, LANES))


def _layer_weights(p, layer):
    (mix_norm, w_in, cq_norm, ckv_norm, w_uq, w_ukv, q_norm, k_norm, conv_w, conv_b,
     filt_w1, filt_b1, filt_freq, filt_w2, filt_b2, filt_w3, filt_decay, hyena_d,
     out_norm, w_out, ffn_norm) = [a[layer] for a in p]
    na = Q_LORA + KV_LORA + QK_ROPE
    ukv = w_ukv.reshape(KV_LORA, N_HEADS, QK_NOPE + V_DIM)
    eye = jnp.eye(QK_ROPE, dtype=F32)
    p_kr = jnp.pad(eye, ((0, 0), (QK_NOPE, HEAD_PAD - QK_DIM)))
    row = lambda a: a.reshape(1, -1).astype(F32)
    col = lambda a: a.reshape(-1, 1).astype(F32)
    w_out_a = w_out[:ATTN_W].reshape(N_HEADS, V_DIM, D_MODEL)
    w_out_a = jnp.pad(w_out_a, ((0, 0), (0, HEAD_PAD - V_DIM), (0, 0))).reshape(PAD_W, D_MODEL)
    w3 = filt_w3.reshape(FILT_HID, 2, 2, HY_W).transpose(2, 1, 3, 0).reshape(2, 2 * HY_W, FILT_HID)
    dec = filt_decay.reshape(2, 2, HY_W).transpose(1, 0, 2).reshape(2, 2 * HY_W, 1)
    return dict(
        mix_norm=row(mix_norm), w_a=w_in[:, :na].astype(BF), w_hy_t=w_in[:, na:].T.astype(BF),
        cq_norm=row(cq_norm), ckv_norm=row(ckv_norm),
        w_uq=_pad_heads(w_uq, QK_DIM).astype(BF),
        w_uk=_pad_heads(ukv[:, :, :QK_NOPE].reshape(KV_LORA, -1), QK_NOPE).astype(BF),
        w_uv=_pad_heads(ukv[:, :, QK_NOPE:].reshape(KV_LORA, -1), V_DIM).astype(BF),
        p_kr=jnp.tile(p_kr, (1, N_HEADS)).astype(BF),
        gq=jnp.tile(jnp.pad(q_norm, (0, HEAD_PAD - QK_DIM)), N_HEADS).reshape(1, PAD_W),
        gk=jnp.tile(jnp.pad(k_norm, (0, HEAD_PAD - QK_DIM)), N_HEADS).reshape(1, PAD_W),
        conv_w_t=_per_channel(conv_w.reshape(3, 3, HY_W)), conv_b_t=_per_channel(conv_b.reshape(3, HY_W)),
        filt_w1_t=jnp.pad(filt_w1, ((0, LANES - POS_EMB), (0, 0))).T, filt_b1=col(filt_b1),
        filt_freq=col(filt_freq), filt_w2_t=filt_w2.T, filt_b2=col(filt_b2), filt_w3_t=w3, filt_decay_t=dec,
        hyena_d_t=_per_channel(hyena_d),
        ga=_pad_heads(out_norm[:ATTN_W], V_DIM).reshape(1, PAD_W),
        gh_t=jnp.broadcast_to(col(out_norm[ATTN_W:]), (HY_W, LANES)),
        w_out_a=w_out_a.astype(BF), w_out_h=w_out[ATTN_W:].astype(BF),
        ffn_norm=row(ffn_norm),
    )


def _position_tables(L):
    pos = jnp.arange(L, dtype=F32)
    half = QK_ROPE // 2
    inv = ROPE_THETA ** (-jnp.arange(0, QK_ROPE, 2, dtype=F32) / QK_ROPE)
    ang = pos[:, None] * inv[None, :]
    cos, sin = jnp.cos(ang), jnp.sin(ang)
    zeros = lambda w: jnp.zeros((L, w), F32)
    cos_t = jnp.concatenate([jnp.ones((L, QK_NOPE), F32), cos, cos, zeros(HEAD_PAD - QK_DIM)], 1)
    sin_a = jnp.concatenate([zeros(QK_NOPE), -sin, zeros(HEAD_PAD - QK_NOPE - half)], 1)
    sin_b = jnp.concatenate([zeros(QK_NOPE + half), sin, zeros(HEAD_PAD - QK_DIM)], 1)
    n = jnp.arange(2 * L, dtype=jnp.int32)
    p = jnp.where(n < L, n, 2 * L - n)
    p = jnp.where(n == L, 0, p)
    t = jnp.take(jnp.linspace(0.0, 1.0, L, dtype=F32), p)[None, :]
    w = 2.0 * math.pi * p.astype(F32) / L
    f = jnp.linspace(1e-4, N_BANDS - 1, N_BANDS, dtype=F32)
    a = f[:, None] * w[None, :]
    zt = jnp.concatenate([t, jnp.cos(a), -jnp.sin(a), jnp.zeros((LANES - POS_EMB, 2 * L), F32)], axis=0)
    return (cos_t, sin_a, sin_b), zt


def _moe_capacity(tm):
    return max(16, (tm * 5 // 16) // 16 * 16)


def _trunk(x, layers, mixers, plan, rope, zt):
    B, L, _ = x.shape
    T = B * L
    x2d = x.reshape(T, D_MODEL)
    tm = _pick(L, 512)
    for lw, mixer in zip(layers, mixers):
        q, k, v = _in_proj(x2d, L, lw, rope, tm)
        hv, hx1, hx2 = _hy_proj(x2d, B, L, lw, _pick(L, 2048))
        attn = _attn(q, k, v, B, L, _pick(L, ATTN_TQ), 1 if L * HEAD_PAD * 2 > ATTN_KV_BYTES else 2)
        taps = _filter_taps(zt, lw, L, _pick(L, 1024))
        hy = _hyena(hv, hx1, hx2, taps, lw, plan, B, L)
        x2d = _out_proj(x2d, attn, hy, lw, B, L, tm)
        tf = 1408
        if mixer[0] == 'dense':
            _, wg, wu, wd = mixer
            x2d = _ffn(x2d, lw['ffn_norm'], wg, wu, wd, _pick(T, 1024), tf)
        else:
            _, rwt, wg, wu, wd = mixer
            tme = _pick(T, 1024)
            x2d = _moe(x2d, lw['ffn_norm'], rwt, wg, wu, wd, tme, tf, _moe_capacity(tme))
    return x2d.reshape(B, L, D_MODEL)


def kernel(x_prompt, x_sample, mix_norm, w_in, cq_norm, ckv_norm, w_uq, w_ukv, q_norm, k_norm, conv_w, conv_b, filt_w1, filt_b1, filt_freq, filt_w2, filt_b2, filt_w3, filt_decay, hyena_d, out_norm, w_out, ffn_norm, dense_wg, dense_wu, dense_wd, router_w, moe_wg, moe_wu, moe_wd):
    per_layer = (mix_norm, w_in, cq_norm, ckv_norm, w_uq, w_ukv, q_norm, k_norm, conv_w, conv_b,
                 filt_w1, filt_b1, filt_freq, filt_w2, filt_b2, filt_w3, filt_decay, hyena_d,
                 out_norm, w_out, ffn_norm)
    depth = mix_norm.shape[0]
    layers = [_layer_weights(per_layer, l) for l in range(depth)]
    mixers = []
    for l in range(depth):
        j = l // 2
        if l % 2 == 0:
            mixers.append(('dense', dense_wg[j].astype(BF), dense_wu[j].astype(BF), dense_wd[j].astype(BF)))
        else:
            mixers.append(('moe', router_w[j].T, moe_wg[j].astype(BF), moe_wu[j].astype(BF), moe_wd[j].astype(BF)))
    outs = []
    for x in (x_prompt, x_sample):
        L = x.shape[1]
        rope, zt = _position_tables(L)
        outs.append(_trunk(x, layers, mixers, _fft_plan(L), rope, zt))
    return tuple(outs)
```

```python
import functools
import math

import jax
import jax.numpy as jnp
from jax import lax
from jax.experimental import pallas as pl
from jax.experimental.pallas import tpu as pltpu

BF = jnp.bfloat16
F32 = jnp.float32

D_MODEL = 1024
N_HEADS = 8
QK_NOPE = 64
QK_ROPE = 32
QK_DIM = QK_NOPE + QK_ROPE
V_DIM = 64
Q_LORA = 256
KV_LORA = 128
ATTN_W = N_HEADS * V_DIM
HY_W = D_MODEL - ATTN_W
ROPE_THETA = 10000.0
N_BANDS = 16
POS_EMB = 1 + 2 * N_BANDS
FILT_HID = 64
N_EXPERTS = 8
EPS = 1e-6

LANES = 128
HEAD_PAD = LANES
PAD_W = N_HEADS * HEAD_PAD
VMEM_LIMIT = 56 * 1024 * 1024
HI = lax.Precision.HIGHEST
ATTN_TQ = 256
ATTN_KV_BYTES = 8 * 1024 * 1024
HY_CB = 16
HY_DT = F32


def _cparams(sem):
    return pltpu.CompilerParams(dimension_semantics=sem, vmem_limit_bytes=VMEM_LIMIT)


def _rms(x, g):
    ms = jnp.mean(x * x, axis=-1, keepdims=True)
    return x * lax.rsqrt(ms + EPS) * g


def _dot(a, b):
    return jnp.dot(a, b, preferred_element_type=F32)


def _full(a):
    return pl.BlockSpec(a.shape, lambda *_: (0,) * a.ndim)


def _in_proj_kernel(x_ref, gmix_ref, wa_ref, cqn_ref, ckvn_ref, wuq_ref, wuk_ref, wuv_ref, pkr_ref,
                    gq_ref, gk_ref, cos_ref, sina_ref, sinb_ref, q_ref, k_ref, v_ref):
    h = _rms(x_ref[...], gmix_ref[...]).astype(BF)
    pa = _dot(h, wa_ref[...])
    c_q = pa[:, 0:Q_LORA]
    c_kv = pa[:, Q_LORA:Q_LORA + KV_LORA]
    k_r = pa[:, Q_LORA + KV_LORA:Q_LORA + KV_LORA + QK_ROPE]
    cqn = _rms(c_q, cqn_ref[...]).astype(BF)
    ckvn = _rms(c_kv, ckvn_ref[...]).astype(BF)
    q = _dot(cqn, wuq_ref[...])
    k = _dot(ckvn, wuk_ref[...]) + _dot(k_r.astype(BF), pkr_ref[...])
    lane = lax.broadcasted_iota(jnp.int32, (1, PAD_W), 1)
    ones_col = jnp.where(lane % HEAD_PAD == V_DIM, 1.0, 0.0)
    v_ref[...] = (_dot(ckvn, wuv_ref[...]) + ones_col).astype(BF)
    cos = cos_ref[...]
    sa = sina_ref[...]
    sb = sinb_ref[...]
    half = QK_ROPE // 2
    scale = QK_DIM ** -0.5 * math.log2(math.e)
    for hh in range(N_HEADS):
        sl = slice(HEAD_PAD * hh, HEAD_PAD * (hh + 1))
        for src, gref, oref, sc in ((q, gq_ref, q_ref, scale), (k, gk_ref, k_ref, 1.0)):
            t = src[:, sl]
            ms = jnp.sum(t * t, axis=-1, keepdims=True) * (1.0 / QK_DIM)
            tn = t * lax.rsqrt(ms + EPS) * gref[:, sl]
            tr = tn * cos + pltpu.roll(tn, HEAD_PAD - half, 1) * sa + pltpu.roll(tn, half, 1) * sb
            oref[:, sl] = (tr * sc).astype(BF)


def _in_proj(x2d, L, lw, rope, tm):
    T = x2d.shape[0]
    tps = L // tm
    weights = [lw['mix_norm'], lw['w_a'], lw['cq_norm'], lw['ckv_norm'],
               lw['w_uq'], lw['w_uk'], lw['w_uv'], lw['p_kr'], lw['gq'], lw['gk']]
    in_specs = ([pl.BlockSpec((tm, D_MODEL), lambda i: (i, 0))] + [_full(w) for w in weights]
                + [pl.BlockSpec((tm, LANES), lambda i: (i % tps, 0))] * 3)
    return pl.pallas_call(
        _in_proj_kernel,
        out_shape=[jax.ShapeDtypeStruct((T, PAD_W), BF)] * 3, grid=(T // tm,), in_specs=in_specs,
        out_specs=[pl.BlockSpec((tm, PAD_W), lambda i: (i, 0))] * 3,
        compiler_params=_cparams(("parallel",)), name="in_proj",
    )(x2d, *weights, *rope)


def _hy_proj_kernel(x_ref, gmix_ref, wt_ref, v_ref, x1_ref, x2_ref):
    h = _rms(x_ref[...], gmix_ref[...]).astype(BF)
    for g, oref in enumerate((v_ref, x1_ref, x2_ref)):
        ut = lax.dot_general(wt_ref[g * HY_W:(g + 1) * HY_W, :], h, (((1,), (1,)), ((), ())),
                             preferred_element_type=F32)
        for a in range(oref.shape[1]):
            oref[:, a, :] = ut[:, a * LANES:(a + 1) * LANES].astype(oref.dtype)


def _hy_proj(x2d, B, L, lw, tm):
    tps = L // tm
    rows = tm // LANES
    shape = jax.ShapeDtypeStruct((B, HY_W, L // LANES, LANES), HY_DT)
    spec = pl.BlockSpec((None, HY_W, rows, LANES), lambda i: (i // tps, 0, i % tps, 0))
    return pl.pallas_call(
        _hy_proj_kernel, out_shape=[shape] * 3, grid=(B * tps,),
        in_specs=[pl.BlockSpec((tm, D_MODEL), lambda i: (i, 0)), _full(lw['mix_norm']), _full(lw['w_hy_t'])],
        out_specs=[spec] * 3,
        compiler_params=_cparams(("parallel",)), name="hy_proj",
    )(x2d, lw['mix_norm'], lw['w_hy_t'])


def _attn_kernel(q_ref, k_ref, v_ref, o_ref, *, heads):
    lane = lax.broadcasted_iota(jnp.int32, (1, HEAD_PAD), 1)
    for h in range(heads):
        sl = slice(HEAD_PAD * h, HEAD_PAD * (h + 1))
        s = lax.dot_general(q_ref[:, sl], k_ref[:, sl], (((1,), (1,)), ((), ())), preferred_element_type=F32)
        m = jnp.max(s, axis=-1, keepdims=True)
        p = jnp.exp2(s - m).astype(BF)
        a = _dot(p, v_ref[:, sl])
        o = jnp.where(lane < V_DIM, a / a[:, V_DIM:V_DIM + 1], 0.0)
        o_ref[:, sl] = o.astype(o_ref.dtype)


def _attn(q, k, v, B, L, tq):
    heads = max(h for h in (1, 2, 4) if h == 1 or L * h * HEAD_PAD * 2 <= ATTN_KV_BYTES)
    kv_mode = dict(pipeline_mode=pl.Buffered(1)) if L * heads * HEAD_PAD * 2 > ATTN_KV_BYTES // 2 else {}
    nq = L // tq
    w = heads * HEAD_PAD
    return pl.pallas_call(
        functools.partial(_attn_kernel, heads=heads),
        out_shape=jax.ShapeDtypeStruct(q.shape, BF),
        grid=(B, N_HEADS // heads, nq),
        in_specs=[
            pl.BlockSpec((tq, w), lambda b, h, qi: (b * nq + qi, h)),
            pl.BlockSpec((L, w), lambda b, h, qi: (b, h), **kv_mode),
            pl.BlockSpec((L, w), lambda b, h, qi: (b, h), **kv_mode),
        ],
        out_specs=pl.BlockSpec((tq, w), lambda b, h, qi: (b * nq + qi, h)),
        compiler_params=_cparams(("parallel", "parallel", "parallel")), name="attn",
    )(q, k, v)


def _filter_kernel(z_ref, w1_ref, b1_ref, fr_ref, w2_ref, b2_ref, w3_ref, dec_ref, o_ref, *, seq_len):
    z = z_ref[...]
    fr = fr_ref[...]
    h = jnp.sin(fr * (jnp.dot(w1_ref[...], z, precision=HI, preferred_element_type=F32) + b1_ref[...]))
    h = jnp.sin(fr * (jnp.dot(w2_ref[...], h, precision=HI, preferred_element_type=F32) + b2_ref[...]))
    t = z[0:1, :]
    o = jnp.dot(w3_ref[...], h, precision=HI, preferred_element_type=F32) * jnp.exp(-t * jnp.abs(dec_ref[...]))
    n = pl.program_id(0) * z.shape[1] + lax.broadcasted_iota(jnp.int32, (1, z.shape[1]), 1)
    o = jnp.where(n == seq_len, 0.0, o)
    for a in range(o_ref.shape[1]):
        o_ref[:, a, :] = o[:, a * LANES:(a + 1) * LANES]


def _filter_taps(zt, lw, L, tl):
    n = 2 * L
    ch = lw['filt_w3_t'].shape[1]
    half_tiles = L // tl
    sel = lambda i: (i // half_tiles, 0, 0)
    ws = [lw['filt_w1_t'], lw['filt_b1'], lw['filt_freq'], lw['filt_w2_t'], lw['filt_b2']]
    return pl.pallas_call(
        functools.partial(_filter_kernel, seq_len=L),
        out_shape=jax.ShapeDtypeStruct((ch, n // LANES, LANES), F32),
        grid=(n // tl,),
        in_specs=[pl.BlockSpec((LANES, tl), lambda i: (0, i))] + [_full(w) for w in ws]
        + [pl.BlockSpec((None, ch, FILT_HID), sel), pl.BlockSpec((None, ch, 1), sel)],
        out_specs=pl.BlockSpec((ch, tl // LANES, LANES), lambda i: (0, i, 0)),
        compiler_params=_cparams(("parallel",)), name="filt_taps",
    )(zt, *ws, lw['filt_w3_t'], lw['filt_decay_t'])


def _lanes_of(parts):
    return parts[0] if len(parts) == 1 else jnp.concatenate(parts, axis=1)


def _hyena_kernel(v_ref, x1_ref, x2_ref, kt_ref, cw_ref, cb_ref, d_ref, msig_ref, mfilt_ref, minv_ref,
                  g2f_ref, g2i_ref, twr_ref, twi_ref, o_ref, kf_sc, *, scale):
    _, cb, nh, _ = v_ref.shape
    n1 = 2 * nh
    twr = twr_ref[...]
    twi = twi_ref[...]

    def fwd(s_all, m):
        a = _dot(m, s_all)
        rows = []
        for c in range(cb):
            ar = a[:n1, c * LANES:(c + 1) * LANES]
            ai = a[n1:, c * LANES:(c + 1) * LANES]
            rows.append(jnp.concatenate([ar * twr - ai * twi, ar * twi + ai * twr], axis=1).astype(BF))
        return _dot(jnp.concatenate(rows, axis=0), g2f_ref[...])

    def inv(z):
        b = _dot(z.astype(BF), g2i_ref[...])
        cols = []
        for c in range(cb):
            br = b[c * n1:(c + 1) * n1, :LANES]
            bi = b[c * n1:(c + 1) * n1, LANES:]
            cols.append(jnp.concatenate([br * twr + bi * twi, bi * twr - br * twi], axis=0).astype(BF))
        return _dot(minv_ref[...], _lanes_of(cols))

    def cmul(x, kf):
        xr, xi = x[:, :LANES], x[:, LANES:]
        kr, ki = kf[:, :LANES], kf[:, LANES:]
        return jnp.concatenate([xr * kr - xi * ki, xr * ki + xi * kr], axis=1)

    @pl.when(pl.program_id(1) == 0)
    def _():
        for o in range(2):
            taps = _lanes_of([kt_ref[o, c] for c in range(cb)]).astype(BF)
            kf_sc[o] = fwd(taps, mfilt_ref[...]) * scale

    row = lax.broadcasted_iota(jnp.int32, (cb * nh, LANES), 0) % nh
    lane = lax.broadcasted_iota(jnp.int32, (cb * nh, LANES), 1)

    def short_conv(ref, g):
        w = [jnp.broadcast_to(cw_ref[k, g], (cb, nh, LANES)).reshape(cb * nh, LANES) for k in range(3)]
        bias = jnp.broadcast_to(cb_ref[g], (cb, nh, LANES)).reshape(cb * nh, LANES)
        halves = []
        for r in range(2):
            u = ref[r].astype(F32).reshape(cb * nh, LANES)
            lp = pltpu.roll(u, 1, 1)
            prev = jnp.where(lane == 0, pltpu.roll(lp, 1, 0), lp)
            prev = jnp.where((lane == 0) & (row == 0), 0.0, prev)
            ln = pltpu.roll(u, LANES - 1, 1)
            nxt = jnp.where(lane == LANES - 1, pltpu.roll(ln, cb * nh - 1, 0), ln)
            nxt = jnp.where((lane == LANES - 1) & (row == nh - 1), 0.0, nxt)
            y = prev * w[0] + u * w[1] + nxt * w[2] + bias
            halves.append(_lanes_of([y[c * nh:(c + 1) * nh] for c in range(cb)]))
        return jnp.concatenate(halves, axis=0)

    def skip(o):
        return _lanes_of([d_ref[o, c] for c in range(cb)])

    v = short_conv(v_ref, 0)
    x1 = short_conv(x1_ref, 1)
    x2 = short_conv(x2_ref, 2)
    y = inv(cmul(fwd(v.astype(BF), msig_ref[...]), kf_sc[0]))
    z1 = (x1 * (y + skip(0) * v)).astype(BF)
    y = inv(cmul(fwd(z1, msig_ref[...]), kf_sc[1]))
    hy = x2 * (y + skip(1) * z1.astype(F32))
    for c in range(cb):
        o_ref[0, :, c, :] = hy[:nh, c * LANES:(c + 1) * LANES].astype(o_ref.dtype)
        o_ref[1, :, c, :] = hy[nh:, c * LANES:(c + 1) * LANES].astype(o_ref.dtype)


def _hyena(hv, hx1, hx2, taps, lw, plan, B, L):
    nh = L // LANES
    n1 = 2 * nh
    P = B // 2
    cb = HY_CB
    pair = lambda a: a.reshape(P, 2, HY_W, nh, LANES)
    sig = pl.BlockSpec((None, 2, cb, nh, LANES), lambda j, p: (p, 0, j, 0, 0))
    consts = [plan['m_sig'], plan['m_filt'], plan['m_inv'], plan['g2f'], plan['g2i'], plan['twr'], plan['twi']]
    out = pl.pallas_call(
        functools.partial(_hyena_kernel, scale=1.0 / (2 * L)),
        out_shape=jax.ShapeDtypeStruct((P, 2, nh, HY_W, LANES), HY_DT),
        grid=(HY_W // cb, P),
        in_specs=[sig, sig, sig,
                  pl.BlockSpec((2, cb, n1, LANES), lambda j, p: (0, j, 0, 0)),
                  pl.BlockSpec((3, 3, cb, 1, LANES), lambda j, p: (0, 0, j, 0, 0)),
                  pl.BlockSpec((3, cb, 1, LANES), lambda j, p: (0, j, 0, 0)),
                  pl.BlockSpec((2, cb, 1, LANES), lambda j, p: (0, j, 0, 0))] + [_full(c) for c in consts],
        out_specs=pl.BlockSpec((None, 2, nh, cb, LANES), lambda j, p: (p, 0, 0, j, 0)),
        scratch_shapes=[pltpu.VMEM((2, cb * n1, 2 * LANES), F32)],
        compiler_params=_cparams(("parallel", "arbitrary")), name="hyena",
    )(pair(hv), pair(hx1), pair(hx2), taps.reshape(2, HY_W, n1, LANES), lw['conv_w_t'], lw['conv_b_t'],
      lw['hyena_d_t'], *consts)
    return out.reshape(B, nh, HY_W, LANES)


def _fft_plan(L):
    n = 2 * L
    n2 = LANES
    n1 = n // n2
    i1 = jnp.arange(n1, dtype=jnp.int32)
    i2 = jnp.arange(n2, dtype=jnp.int32)
    a1 = (2.0 * math.pi / n1) * ((i1[:, None] * i1[None, :]) % n1).astype(F32)
    c1, s1 = jnp.cos(a1), jnp.sin(a1)
    hn = n1 // 2
    m_sig = jnp.concatenate([jnp.concatenate([c1[:, :hn], s1[:, :hn]], 1),
                             jnp.concatenate([-s1[:, :hn], c1[:, :hn]], 1)], 0).astype(BF)
    m_filt = jnp.concatenate([c1, -s1], 0).astype(BF)
    m_inv = jnp.concatenate([jnp.concatenate([c1[:hn], -s1[:hn]], 1),
                             jnp.concatenate([s1[:hn], c1[:hn]], 1)], 0).astype(BF)
    a2 = (2.0 * math.pi / n2) * ((i2[:, None] * i2[None, :]) % n2).astype(F32)
    c2, s2 = jnp.cos(a2), jnp.sin(a2)
    g2f = jnp.concatenate([jnp.concatenate([c2, -s2], 1), jnp.concatenate([s2, c2], 1)], 0).astype(BF)
    g2i = jnp.concatenate([jnp.concatenate([c2, s2], 1), jnp.concatenate([-s2, c2], 1)], 0).astype(BF)
    at = (2.0 * math.pi / n) * (i1[:, None] * i2[None, :]).astype(F32)
    return dict(m_sig=m_sig, m_filt=m_filt, m_inv=m_inv, g2f=g2f, g2i=g2i, twr=jnp.cos(at), twi=-jnp.sin(at))


def _pick(total, want):
    b = min(total, want)
    while total % b:
        b //= 2
    return b


def _out_proj_kernel(x_ref, a_ref, hy_ref, ga_ref, gh_ref, wa_ref, wh_ref, o_ref):
    a = a_ref[...].astype(F32)
    ra = lax.rsqrt(jnp.sum(a * a, axis=-1, keepdims=True) * (1.0 / ATTN_W) + EPS)
    an = (a * ra * ga_ref[...]).astype(BF)
    acc = x_ref[...] + _dot(an, wa_ref[...])
    gh = gh_ref[...]
    wh = wh_ref[...]
    for s in range(hy_ref.shape[0]):
        hy = hy_ref[s].astype(F32)
        rh = lax.rsqrt(jnp.sum(hy * hy, axis=0, keepdims=True) * (1.0 / HY_W) + EPS)
        hn = (hy * rh * gh).astype(BF)
        part = lax.dot_general(hn, wh, (((0,), (0,)), ((), ())), preferred_element_type=F32)
        o_ref[s * LANES:(s + 1) * LANES, :] = acc[s * LANES:(s + 1) * LANES, :] + part


def _out_proj(x2d, attn, hy, lw, B, L, tm):
    T = x2d.shape[0]
    tps = L // tm
    ws = [lw['ga'], lw['gh_t'], lw['w_out_a'], lw['w_out_h']]
    return pl.pallas_call(
        _out_proj_kernel,
        out_shape=jax.ShapeDtypeStruct((T, D_MODEL), F32),
        grid=(T // tm,),
        in_specs=[pl.BlockSpec((tm, D_MODEL), lambda i: (i, 0)),
                  pl.BlockSpec((tm, PAD_W), lambda i: (i, 0)),
                  pl.BlockSpec((None, tm // LANES, HY_W, LANES), lambda i: (i // tps, i % tps, 0, 0))]
        + [_full(w) for w in ws],
        out_specs=pl.BlockSpec((tm, D_MODEL), lambda i: (i, 0)),
        compiler_params=_cparams(("parallel",)), name="out_proj",
    )(x2d, attn, hy, *ws)


def _silu_mul(g, u):
    return g * (1.0 / (1.0 + jnp.exp(-g))) * u


def _ffn_kernel(x_ref, g_ref, wg_ref, wu_ref, wd_ref, o_ref, hn_sc, acc_sc):
    j = pl.program_id(1)

    @pl.when(j == 0)
    def _():
        hn_sc[...] = _rms(x_ref[...], g_ref[...]).astype(BF)
        acc_sc[...] = jnp.zeros(acc_sc.shape, F32)

    hn = hn_sc[...]
    a = _silu_mul(_dot(hn, wg_ref[...]), _dot(hn, wu_ref[...])).astype(BF)
    acc_sc[...] += _dot(a, wd_ref[...])

    @pl.when(j == pl.num_programs(1) - 1)
    def _():
        o_ref[...] = x_ref[...] + acc_sc[...]


def _ffn(x2d, g, wg, wu, wd, tm, tf):
    T = x2d.shape[0]
    ff = wg.shape[1]
    return pl.pallas_call(
        _ffn_kernel,
        out_shape=jax.ShapeDtypeStruct((T, D_MODEL), F32),
        grid=(T // tm, ff // tf),
        in_specs=[pl.BlockSpec((tm, D_MODEL), lambda i, j: (i, 0)),
                  pl.BlockSpec((1, D_MODEL), lambda i, j: (0, 0)),
                  pl.BlockSpec((D_MODEL, tf), lambda i, j: (0, j)),
                  pl.BlockSpec((D_MODEL, tf), lambda i, j: (0, j)),
                  pl.BlockSpec((tf, D_MODEL), lambda i, j: (j, 0))],
        out_specs=pl.BlockSpec((tm, D_MODEL), lambda i, j: (i, 0)),
        scratch_shapes=[pltpu.VMEM((tm, D_MODEL), BF), pltpu.VMEM((tm, D_MODEL), F32)],
        compiler_params=_cparams(("parallel", "arbitrary")), name="ffn",
    )(x2d, g, wg, wu, wd)


def _moe_kernel(x_ref, g_ref, rwt_ref, wg_ref, wu_ref, wd_ref, o_ref, hn_sc, sel_sc, pos_sc, gate_sc, acc_sc,
                oh_sc, g_sc, xg_sc, y_sc, *, cap):
    e = pl.program_id(1)
    j = pl.program_id(2)
    tm = x_ref.shape[0]

    @pl.when((e == 0) & (j == 0))
    def _():
        hn = _rms(x_ref[...], g_ref[...])
        hn_sc[...] = hn.astype(BF)
        lg = lax.dot_general(rwt_ref[...], hn, (((1,), (1,)), ((), ())), precision=HI, preferred_element_type=F32)
        row = lax.broadcasted_iota(jnp.int32, lg.shape, 0)
        m1 = jnp.max(lg, axis=0, keepdims=True)
        i1 = jnp.min(jnp.where(lg == m1, row, N_EXPERTS), axis=0, keepdims=True)
        lg2 = jnp.where(row == i1, -jnp.inf, lg)
        m2 = jnp.max(lg2, axis=0, keepdims=True)
        i2 = jnp.min(jnp.where(lg2 == m2, row, N_EXPERTS), axis=0, keepdims=True)
        e2 = jnp.exp(m2 - m1)
        g1 = 1.0 / (1.0 + e2)
        gate_sc[...] = jnp.where(row == i1, g1, 0.0) + jnp.where(row == i2, e2 * g1, 0.0)
        sel = jnp.where((row == i1) | (row == i2), 1.0, 0.0)
        sel_sc[...] = sel
        r = lax.broadcasted_iota(jnp.int32, (tm, tm), 0)
        c = lax.broadcasted_iota(jnp.int32, (tm, tm), 1)
        pos_sc[...] = _dot(sel.astype(BF), jnp.where(r < c, 1.0, 0.0).astype(BF))
        acc_sc[...] = jnp.zeros(acc_sc.shape, F32)

    sel_row = sel_sc[pl.ds(e, 1), :]
    pos_row = pos_sc[pl.ds(e, 1), :]
    gate_row = gate_sc[pl.ds(e, 1), :]
    count = jnp.sum(sel_row).astype(jnp.int32)
    slot = lax.broadcasted_iota(jnp.int32, (cap, tm), 0).astype(F32)
    last_j = pl.num_programs(2) - 1

    def one_hot(c):
        base = jnp.asarray(c * cap).astype(F32)
        onehot = jnp.where((pos_row - base == slot) & (sel_row > 0.0), 1.0, 0.0)
        gate = jnp.sum(onehot * gate_row, axis=1, keepdims=True)
        return onehot.astype(BF), gate

    def expert(xg):
        h = _silu_mul(_dot(xg, wg_ref[...]), _dot(xg, wu_ref[...])).astype(BF)
        return _dot(h, wd_ref[...])

    def scatter(onehot, y):
        acc_sc[...] += lax.dot_general(onehot, y.astype(BF), (((0,), (0,)), ((), ())), preferred_element_type=F32)

    @pl.when(j == 0)
    def _():
        onehot, gate = one_hot(0)
        oh_sc[...] = onehot
        g_sc[...] = gate
        xg_sc[...] = _dot(onehot, hn_sc[...]).astype(BF)
        y_sc[...] = jnp.zeros(y_sc.shape, F32)

    y_sc[...] += expert(xg_sc[...])

    @pl.when(j == last_j)
    def _():
        scatter(oh_sc[...], y_sc[...] * g_sc[...])

    def chunk(c, carry):
        onehot, gate = one_hot(c)
        xg = _dot(onehot, hn_sc[...]).astype(BF)
        scatter(onehot, expert(xg) * gate)
        return carry

    lax.fori_loop(1, (count + cap - 1) // cap, chunk, 0)

    @pl.when((e == pl.num_programs(1) - 1) & (j == last_j))
    def _():
        o_ref[...] = x_ref[...] + acc_sc[...]


def _moe(x2d, g, rwt, wg, wu, wd, tm, tf, cap):
    T = x2d.shape[0]
    ne, _, ff = wg.shape
    return pl.pallas_call(
        functools.partial(_moe_kernel, cap=cap),
        out_shape=jax.ShapeDtypeStruct((T, D_MODEL), F32),
        grid=(T // tm, ne, ff // tf),
        in_specs=[pl.BlockSpec((tm, D_MODEL), lambda i, e, j: (i, 0)),
                  pl.BlockSpec((1, D_MODEL), lambda i, e, j: (0, 0)),
                  pl.BlockSpec((ne, D_MODEL), lambda i, e, j: (0, 0)),
                  pl.BlockSpec((None, D_MODEL, tf), lambda i, e, j: (e, 0, j)),
                  pl.BlockSpec((None, D_MODEL, tf), lambda i, e, j: (e, 0, j)),
                  pl.BlockSpec((None, tf, D_MODEL), lambda i, e, j: (e, j, 0))],
        out_specs=pl.BlockSpec((tm, D_MODEL), lambda i, e, j: (i, 0)),
        scratch_shapes=[pltpu.VMEM((tm, D_MODEL), BF), pltpu.VMEM((ne, tm), F32), pltpu.VMEM((ne, tm), F32),
                        pltpu.VMEM((ne, tm), F32), pltpu.VMEM((tm, D_MODEL), F32),
                        pltpu.VMEM((cap, tm), BF), pltpu.VMEM((cap, 1), F32),
                        pltpu.VMEM((cap, D_MODEL), BF), pltpu.VMEM((cap, D_MODEL), F32)],
        compiler_params=_cparams(("parallel", "arbitrary", "arbitrary")), name="moe",
    )(x2d, g, rwt, wg, wu, wd)


def _pad_heads(w, d):
    lead = w.shape[:-1]
    w = w.reshape(lead + (N_HEADS, d))
    w = jnp.pad(w, [(0, 0)] * len(lead) + [(0, 0), (0, HEAD_PAD - d)])
    return w.reshape(lead + (PAD_W,))


def _per_channel(a):
    return jnp.broadcast_to(a.astype(F32)[..., None, None], a.shape + (1, LANES))


def _layer_weights(p, layer):
    (mix_norm, w_in, cq_norm, ckv_norm, w_uq, w_ukv, q_norm, k_norm, conv_w, conv_b,
     filt_w1, filt_b1, filt_freq, filt_w2, filt_b2, filt_w3, filt_decay, hyena_d,
     out_norm, w_out, ffn_norm) = [a[layer] for a in p]
    na = Q_LORA + KV_LORA + QK_ROPE
    ukv = w_ukv.reshape(KV_LORA, N_HEADS, QK_NOPE + V_DIM)
    eye = jnp.eye(QK_ROPE, dtype=F32)
    p_kr = jnp.pad(eye, ((0, 0), (QK_NOPE, HEAD_PAD - QK_DIM)))
    row = lambda a: a.reshape(1, -1).astype(F32)
    col = lambda a: a.reshape(-1, 1).astype(F32)
    w_out_a = w_out[:ATTN_W].reshape(N_HEADS, V_DIM, D_MODEL)
    w_out_a = jnp.pad(w_out_a, ((0, 0), (0, HEAD_PAD - V_DIM), (0, 0))).reshape(PAD_W, D_MODEL)
    w3 = filt_w3.reshape(FILT_HID, 2, 2, HY_W).transpose(2, 1, 3, 0).reshape(2, 2 * HY_W, FILT_HID)
    dec = filt_decay.reshape(2, 2, HY_W).transpose(1, 0, 2).reshape(2, 2 * HY_W, 1)
    return dict(
        mix_norm=row(mix_norm), w_a=w_in[:, :na].astype(BF), w_hy_t=w_in[:, na:].T.astype(BF),
        cq_norm=row(cq_norm), ckv_norm=row(ckv_norm),
        w_uq=_pad_heads(w_uq, QK_DIM).astype(BF),
        w_uk=_pad_heads(ukv[:, :, :QK_NOPE].reshape(KV_LORA, -1), QK_NOPE).astype(BF),
        w_uv=_pad_heads(ukv[:, :, QK_NOPE:].reshape(KV_LORA, -1), V_DIM).astype(BF),
        p_kr=jnp.tile(p_kr, (1, N_HEADS)).astype(BF),
        gq=jnp.tile(jnp.pad(q_norm, (0, HEAD_PAD - QK_DIM)), N_HEADS).reshape(1, PAD_W),
        gk=jnp.tile(jnp.pad(k_norm, (0, HEAD_PAD - QK_DIM)), N_HEADS).reshape(1, PAD_W),
        conv_w_t=_per_channel(conv_w.reshape(3, 3, HY_W)), conv_b_t=_per_channel(conv_b.reshape(3, HY_W)),
        filt_w1_t=jnp.pad(filt_w1, ((0, LANES - POS_EMB), (0, 0))).T, filt_b1=col(filt_b1),
        filt_freq=col(filt_freq), filt_w2_t=filt_w2.T, filt_b2=col(filt_b2), filt_w3_t=w3, filt_decay_t=dec,
        hyena_d_t=_per_channel(hyena_d),
        ga=_pad_heads(out_norm[:ATTN_W], V_DIM).reshape(1, PAD_W),
        gh_t=jnp.broadcast_to(col(out_norm[ATTN_W:]), (HY_W, LANES)),
        w_out_a=w_out_a.astype(BF), w_out_h=w_out[ATTN_W:].astype(BF),
        ffn_norm=row(ffn_norm),
    )


def _position_tables(L):
    pos = jnp.arange(L, dtype=F32)
    half = QK_ROPE // 2
    inv = ROPE_THETA ** (-jnp.arange(0, QK_ROPE, 2, dtype=F32) / QK_ROPE)
    ang = pos[:, None] * inv[None, :]
    cos, sin = jnp.cos(ang), jnp.sin(ang)
    zeros = lambda w: jnp.zeros((L, w), F32)
    cos_t = jnp.concatenate([jnp.ones((L, QK_NOPE), F32), cos, cos, zeros(HEAD_PAD - QK_DIM)], 1)
    sin_a = jnp.concatenate([zeros(QK_NOPE), -sin, zeros(HEAD_PAD - QK_NOPE - half)], 1)
    sin_b = jnp.concatenate([zeros(QK_NOPE + half), sin, zeros(HEAD_PAD - QK_DIM)], 1)
    n = jnp.arange(2 * L, dtype=jnp.int32)
    p = jnp.where(n < L, n, 2 * L - n)
    p = jnp.where(n == L, 0, p)
    t = jnp.take(jnp.linspace(0.0, 1.0, L, dtype=F32), p)[None, :]
    w = 2.0 * math.pi * p.astype(F32) / L
    f = jnp.linspace(1e-4, N_BANDS - 1, N_BANDS, dtype=F32)
    a = f[:, None] * w[None, :]
    zt = jnp.concatenate([t, jnp.cos(a), -jnp.sin(a), jnp.zeros((LANES - POS_EMB, 2 * L), F32)], axis=0)
    return (cos_t, sin_a, sin_b), zt


def _moe_capacity(tm):
    return max(16, (tm * 9 // 32) // 16 * 16)


def _trunk(x, layers, mixers, plan, rope, zt):
    B, L, _ = x.shape
    T = B * L
    x2d = x.reshape(T, D_MODEL)
    tm = _pick(L, 512)
    for lw, mixer in zip(layers, mixers):
        q, k, v = _in_proj(x2d, L, lw, rope, tm)
        hv, hx1, hx2 = _hy_proj(x2d, B, L, lw, _pick(L, 2048))
        attn = _attn(q, k, v, B, L, _pick(L, ATTN_TQ))
        taps = _filter_taps(zt, lw, L, _pick(L, 1024))
        hy = _hyena(hv, hx1, hx2, taps, lw, plan, B, L)
        x2d = _out_proj(x2d, attn, hy, lw, B, L, tm)
        tf = 1408
        if mixer[0] == 'dense':
            _, wg, wu, wd = mixer
            x2d = _ffn(x2d, lw['ffn_norm'], wg, wu, wd, _pick(T, 1024), tf)
        else:
            _, rwt, wg, wu, wd = mixer
            tme = _pick(T, 1024)
            x2d = _moe(x2d, lw['ffn_norm'], rwt, wg, wu, wd, tme, tf, _moe_capacity(tme))
    return x2d.reshape(B, L, D_MODEL)


def kernel(x_prompt, x_sample, mix_norm, w_in, cq_norm, ckv_norm, w_uq, w_ukv, q_norm, k_norm, conv_w, conv_b, filt_w1, filt_b1, filt_freq, filt_w2, filt_b2, filt_w3, filt_decay, hyena_d, out_norm, w_out, ffn_norm, dense_wg, dense_wu, dense_wd, router_w, moe_wg, moe_wu, moe_wd):
    per_layer = (mix_norm, w_in, cq_norm, ckv_norm, w_uq, w_ukv, q_norm, k_norm, conv_w, conv_b,
                 filt_w1, filt_b1, filt_freq, filt_w2, filt_b2, filt_w3, filt_decay, hyena_d,
                 out_norm, w_out, ffn_norm)
    depth = mix_norm.shape[0]
    layers = [_layer_weights(per_layer, l) for l in range(depth)]
    mixers = []
    for l in range(depth):
        j = l // 2
        if l % 2 == 0:
            mixers.append(('dense', dense_wg[j].astype(BF), dense_wu[j].astype(BF), dense_wd[j].astype(BF)))
        else:
            mixers.append(('moe', router_w[j].T, moe_wg[j].astype(BF), moe_wu[j].astype(BF), moe_wd[j].astype(BF)))
    outs = []
    for x in (x_prompt, x_sample):
        L = x.shape[1]
        rope, zt = _position_tables(L)
        outs.append(_trunk(x, layers, mixers, _fft_plan(L), rope, zt))
    return tuple(outs)
```

```python
import functools
import math

import jax
import jax.numpy as jnp
from jax import lax
from jax.experimental import pallas as pl
from jax.experimental.pallas import tpu as pltpu

BF = jnp.bfloat16
F32 = jnp.float32

D_MODEL = 1024
N_HEADS = 8
QK_NOPE = 64
QK_ROPE = 32
QK_DIM = QK_NOPE + QK_ROPE
V_DIM = 64
Q_LORA = 256
KV_LORA = 128
ATTN_W = N_HEADS * V_DIM
HY_W = D_MODEL - ATTN_W
ROPE_THETA = 10000.0
N_BANDS = 16
POS_EMB = 1 + 2 * N_BANDS
FILT_HID = 64
N_EXPERTS = 8
EPS = 1e-6

LANES = 128
HEAD_PAD = LANES
PAD_W = N_HEADS * HEAD_PAD
VMEM_LIMIT = 56 * 1024 * 1024
HI = lax.Precision.HIGHEST
ATTN_TQ = 256
ATTN_KV_BYTES = 8 * 1024 * 1024
HY_CB = 16
HY_DT = F32


def _cparams(sem):
    return pltpu.CompilerParams(dimension_semantics=sem, vmem_limit_bytes=VMEM_LIMIT)


def _rms(x, g):
    ms = jnp.mean(x * x, axis=-1, keepdims=True)
    return x * lax.rsqrt(ms + EPS) * g


def _dot(a, b):
    return jnp.dot(a, b, preferred_element_type=F32)


def _full(a):
    return pl.BlockSpec(a.shape, lambda *_: (0,) * a.ndim)


def _in_proj_kernel(x_ref, gmix_ref, wa_ref, cqn_ref, ckvn_ref, wuq_ref, wuqs_ref, wuk_ref, wuv_ref, pkr_ref,
                    pkrs_ref, cq_ref, sq_ref, ck_ref, sk_ref, q_ref, k_ref, v_ref):
    h = _rms(x_ref[...], gmix_ref[...]).astype(BF)
    pa = _dot(h, wa_ref[...])
    c_q = pa[:, 0:Q_LORA]
    c_kv = pa[:, Q_LORA:Q_LORA + KV_LORA]
    k_r = pa[:, Q_LORA + KV_LORA:Q_LORA + KV_LORA + QK_ROPE].astype(BF)
    cqn = _rms(c_q, cqn_ref[...]).astype(BF)
    ckvn = _rms(c_kv, ckvn_ref[...]).astype(BF)
    q = _dot(cqn, wuq_ref[...])
    k = _dot(ckvn, wuk_ref[...]) + _dot(k_r, pkr_ref[...])
    q_sw = _dot(cqn, wuqs_ref[...])
    k_sw = _dot(k_r, pkrs_ref[...])
    lane = lax.broadcasted_iota(jnp.int32, (1, PAD_W), 1)
    ones_col = jnp.where(lane % HEAD_PAD == V_DIM, 1.0, 0.0)
    v_ref[...] = (_dot(ckvn, wuv_ref[...]) + ones_col).astype(BF)
    for src, swp, cref, sref, oref in ((q, q_sw, cq_ref, sq_ref, q_ref), (k, k_sw, ck_ref, sk_ref, k_ref)):
        ct = cref[...]
        st = sref[...]
        for hh in range(N_HEADS):
            sl = slice(HEAD_PAD * hh, HEAD_PAD * (hh + 1))
            t = src[:, sl]
            ms = jnp.sum(t * t, axis=-1, keepdims=True) * (1.0 / QK_DIM)
            oref[:, sl] = ((t * ct + swp[:, sl] * st) * lax.rsqrt(ms + EPS)).astype(BF)


def _in_proj(x2d, L, lw, rope, tm):
    T = x2d.shape[0]
    tps = L // tm
    weights = [lw['mix_norm'], lw['w_a'], lw['cq_norm'], lw['ckv_norm'],
               lw['w_uq'], lw['w_uq_sw'], lw['w_uk'], lw['w_uv'], lw['p_kr'], lw['p_kr_sw']]
    cos_t, sin_t = rope
    tables = [cos_t * lw['gq_cos'], sin_t * lw['gq_sin'], cos_t * lw['gk_cos'], sin_t * lw['gk_sin']]
    in_specs = ([pl.BlockSpec((tm, D_MODEL), lambda i: (i, 0))] + [_full(w) for w in weights]
                + [pl.BlockSpec((tm, LANES), lambda i: (i % tps, 0))] * 4)
    return pl.pallas_call(
        _in_proj_kernel,
        out_shape=[jax.ShapeDtypeStruct((T, PAD_W), BF)] * 3, grid=(T // tm,), in_specs=in_specs,
        out_specs=[pl.BlockSpec((tm, PAD_W), lambda i: (i, 0))] * 3,
        compiler_params=_cparams(("parallel",)), name="in_proj",
    )(x2d, *weights, *tables)


def _hy_proj_kernel(x_ref, gmix_ref, wt_ref, v_ref, x1_ref, x2_ref):
    h = _rms(x_ref[...], gmix_ref[...]).astype(BF)
    for g, oref in enumerate((v_ref, x1_ref, x2_ref)):
        ut = lax.dot_general(wt_ref[g * HY_W:(g + 1) * HY_W, :], h, (((1,), (1,)), ((), ())),
                             preferred_element_type=F32)
        for a in range(oref.shape[1]):
            oref[:, a, :] = ut[:, a * LANES:(a + 1) * LANES].astype(oref.dtype)


def _hy_proj(x2d, B, L, lw, tm):
    tps = L // tm
    rows = tm // LANES
    shape = jax.ShapeDtypeStruct((B, HY_W, L // LANES, LANES), HY_DT)
    spec = pl.BlockSpec((None, HY_W, rows, LANES), lambda i: (i // tps, 0, i % tps, 0))
    return pl.pallas_call(
        _hy_proj_kernel, out_shape=[shape] * 3, grid=(B * tps,),
        in_specs=[pl.BlockSpec((tm, D_MODEL), lambda i: (i, 0)), _full(lw['mix_norm']), _full(lw['w_hy_t'])],
        out_specs=[spec] * 3,
        compiler_params=_cparams(("parallel",)), name="hy_proj",
    )(x2d, lw['mix_norm'], lw['w_hy_t'])


def _attn_kernel(q_ref, k_ref, v_ref, o_ref, *, heads):
    lane = lax.broadcasted_iota(jnp.int32, (1, HEAD_PAD), 1)
    for h in range(heads):
        sl = slice(HEAD_PAD * h, HEAD_PAD * (h + 1))
        s = lax.dot_general(q_ref[:, sl], k_ref[:, sl], (((1,), (1,)), ((), ())), preferred_element_type=F32)
        m = jnp.max(s, axis=-1, keepdims=True)
        p = jnp.exp2(s - m).astype(BF)
        a = _dot(p, v_ref[:, sl])
        o = jnp.where(lane < V_DIM, a / a[:, V_DIM:V_DIM + 1], 0.0)
        o_ref[:, sl] = o.astype(o_ref.dtype)


def _attn(q, k, v, B, L, tq):
    heads = max(h for h in (1, 2, 4) if h == 1 or L * h * HEAD_PAD * 2 <= ATTN_KV_BYTES)
    kv_mode = dict(pipeline_mode=pl.Buffered(1)) if L * heads * HEAD_PAD * 2 > ATTN_KV_BYTES // 2 else {}
    nq = L // tq
    w = heads * HEAD_PAD
    return pl.pallas_call(
        functools.partial(_attn_kernel, heads=heads),
        out_shape=jax.ShapeDtypeStruct(q.shape, BF),
        grid=(B, N_HEADS // heads, nq),
        in_specs=[
            pl.BlockSpec((tq, w), lambda b, h, qi: (b * nq + qi, h)),
            pl.BlockSpec((L, w), lambda b, h, qi: (b, h), **kv_mode),
            pl.BlockSpec((L, w), lambda b, h, qi: (b, h), **kv_mode),
        ],
        out_specs=pl.BlockSpec((tq, w), lambda b, h, qi: (b * nq + qi, h)),
        compiler_params=_cparams(("parallel", "parallel", "parallel")), name="attn",
    )(q, k, v)


def _filter_kernel(z_ref, w1_ref, b1_ref, fr_ref, w2_ref, b2_ref, w3_ref, dec_ref, o_ref, *, seq_len):
    z = z_ref[...]
    fr = fr_ref[...]
    h = jnp.sin(fr * (jnp.dot(w1_ref[...], z, precision=HI, preferred_element_type=F32) + b1_ref[...]))
    h = jnp.sin(fr * (jnp.dot(w2_ref[...], h, precision=HI, preferred_element_type=F32) + b2_ref[...]))
    t = z[0:1, :]
    o = jnp.dot(w3_ref[...], h, precision=HI, preferred_element_type=F32) * jnp.exp(-t * jnp.abs(dec_ref[...]))
    n = pl.program_id(0) * z.shape[1] + lax.broadcasted_iota(jnp.int32, (1, z.shape[1]), 1)
    o = jnp.where(n == seq_len, 0.0, o)
    for a in range(o_ref.shape[1]):
        o_ref[:, a, :] = o[:, a * LANES:(a + 1) * LANES]


def _filter_taps(zt, lw, L, tl):
    n = 2 * L
    ch = lw['filt_w3_t'].shape[1]
    half_tiles = L // tl
    sel = lambda i: (i // half_tiles, 0, 0)
    ws = [lw['filt_w1_t'], lw['filt_b1'], lw['filt_freq'], lw['filt_w2_t'], lw['filt_b2']]
    return pl.pallas_call(
        functools.partial(_filter_kernel, seq_len=L),
        out_shape=jax.ShapeDtypeStruct((ch, n // LANES, LANES), F32),
        grid=(n // tl,),
        in_specs=[pl.BlockSpec((LANES, tl), lambda i: (0, i))] + [_full(w) for w in ws]
        + [pl.BlockSpec((None, ch, FILT_HID), sel), pl.BlockSpec((None, ch, 1), sel)],
        out_specs=pl.BlockSpec((ch, tl // LANES, LANES), lambda i: (0, i, 0)),
        compiler_params=_cparams(("parallel",)), name="filt_taps",
    )(zt, *ws, lw['filt_w3_t'], lw['filt_decay_t'])


def _lanes_of(parts):
    return parts[0] if len(parts) == 1 else jnp.concatenate(parts, axis=1)


def _hyena_kernel(v_ref, x1_ref, x2_ref, kt_ref, cw_ref, cb_ref, d_ref, msig_ref, mfilt_ref, minv_ref,
                  g2f_ref, g2i_ref, twr_ref, twi_ref, o_ref, kf_sc, *, scale):
    _, cb, nh, _ = v_ref.shape
    n1 = 2 * nh
    twr = twr_ref[...]
    twi = twi_ref[...]

    def fwd(s_all, m):
        a = _dot(m, s_all)
        rows = []
        for c in range(cb):
            ar = a[:n1, c * LANES:(c + 1) * LANES]
            ai = a[n1:, c * LANES:(c + 1) * LANES]
            rows.append(jnp.concatenate([ar * twr - ai * twi, ar * twi + ai * twr], axis=1).astype(BF))
        return _dot(jnp.concatenate(rows, axis=0), g2f_ref[...])

    def inv(z):
        b = _dot(z.astype(BF), g2i_ref[...])
        cols = []
        for c in range(cb):
            br = b[c * n1:(c + 1) * n1, :LANES]
            bi = b[c * n1:(c + 1) * n1, LANES:]
            cols.append(jnp.concatenate([br * twr + bi * twi, bi * twr - br * twi], axis=0).astype(BF))
        return _dot(minv_ref[...], _lanes_of(cols))

    def cmul(x, kf):
        xr, xi = x[:, :LANES], x[:, LANES:]
        kr, ki = kf[:, :LANES], kf[:, LANES:]
        return jnp.concatenate([xr * kr - xi * ki, xr * ki + xi * kr], axis=1)

    @pl.when(pl.program_id(1) == 0)
    def _():
        for o in range(2):
            taps = _lanes_of([kt_ref[o, c] for c in range(cb)]).astype(BF)
            kf_sc[o] = fwd(taps, mfilt_ref[...]) * scale

    row = lax.broadcasted_iota(jnp.int32, (cb * nh, LANES), 0) % nh
    lane = lax.broadcasted_iota(jnp.int32, (cb * nh, LANES), 1)

    def short_conv(ref, g):
        w = [jnp.broadcast_to(cw_ref[k, g], (cb, nh, LANES)).reshape(cb * nh, LANES) for k in range(3)]
        bias = jnp.broadcast_to(cb_ref[g], (cb, nh, LANES)).reshape(cb * nh, LANES)
        halves = []
        for r in range(2):
            u = ref[r].astype(F32).reshape(cb * nh, LANES)
            lp = pltpu.roll(u, 1, 1)
            prev = jnp.where(lane == 0, pltpu.roll(lp, 1, 0), lp)
            prev = jnp.where((lane == 0) & (row == 0), 0.0, prev)
            ln = pltpu.roll(u, LANES - 1, 1)
            nxt = jnp.where(lane == LANES - 1, pltpu.roll(ln, cb * nh - 1, 0), ln)
            nxt = jnp.where((lane == LANES - 1) & (row == nh - 1), 0.0, nxt)
            y = prev * w[0] + u * w[1] + nxt * w[2] + bias
            halves.append(_lanes_of([y[c * nh:(c + 1) * nh] for c in range(cb)]))
        return jnp.concatenate(halves, axis=0)

    def skip(o):
        return _lanes_of([d_ref[o, c] for c in range(cb)])

    v = short_conv(v_ref, 0)
    x1 = short_conv(x1_ref, 1)
    x2 = short_conv(x2_ref, 2)
    y = inv(cmul(fwd(v.astype(BF), msig_ref[...]), kf_sc[0]))
    z1 = (x1 * (y + skip(0) * v)).astype(BF)
    y = inv(cmul(fwd(z1, msig_ref[...]), kf_sc[1]))
    hy = x2 * (y + skip(1) * z1.astype(F32))
    for c in range(cb):
        o_ref[0, :, c, :] = hy[:nh, c * LANES:(c + 1) * LANES].astype(o_ref.dtype)
        o_ref[1, :, c, :] = hy[nh:, c * LANES:(c + 1) * LANES].astype(o_ref.dtype)


def _hyena(hv, hx1, hx2, taps, lw, plan, B, L):
    nh = L // LANES
    n1 = 2 * nh
    P = B // 2
    cb = HY_CB
    pair = lambda a: a.reshape(P, 2, HY_W, nh, LANES)
    sig = pl.BlockSpec((None, 2, cb, nh, LANES), lambda j, p: (p, 0, j, 0, 0))
    consts = [plan['m_sig'], plan['m_filt'], plan['m_inv'], plan['g2f'], plan['g2i'], plan['twr'], plan['twi']]
    out = pl.pallas_call(
        functools.partial(_hyena_kernel, scale=1.0 / (2 * L)),
        out_shape=jax.ShapeDtypeStruct((P, 2, nh, HY_W, LANES), HY_DT),
        grid=(HY_W // cb, P),
        in_specs=[sig, sig, sig,
                  pl.BlockSpec((2, cb, n1, LANES), lambda j, p: (0, j, 0, 0)),
                  pl.BlockSpec((3, 3, cb, 1, LANES), lambda j, p: (0, 0, j, 0, 0)),
                  pl.BlockSpec((3, cb, 1, LANES), lambda j, p: (0, j, 0, 0)),
                  pl.BlockSpec((2, cb, 1, LANES), lambda j, p: (0, j, 0, 0))] + [_full(c) for c in consts],
        out_specs=pl.BlockSpec((None, 2, nh, cb, LANES), lambda j, p: (p, 0, 0, j, 0)),
        scratch_shapes=[pltpu.VMEM((2, cb * n1, 2 * LANES), F32)],
        compiler_params=_cparams(("parallel", "arbitrary")), name="hyena",
    )(pair(hv), pair(hx1), pair(hx2), taps.reshape(2, HY_W, n1, LANES), lw['conv_w_t'], lw['conv_b_t'],
      lw['hyena_d_t'], *consts)
    return out.reshape(B, nh, HY_W, LANES)


def _fft_plan(L):
    n = 2 * L
    n2 = LANES
    n1 = n // n2
    i1 = jnp.arange(n1, dtype=jnp.int32)
    i2 = jnp.arange(n2, dtype=jnp.int32)
    a1 = (2.0 * math.pi / n1) * ((i1[:, None] * i1[None, :]) % n1).astype(F32)
    c1, s1 = jnp.cos(a1), jnp.sin(a1)
    hn = n1 // 2
    m_sig = jnp.concatenate([jnp.concatenate([c1[:, :hn], s1[:, :hn]], 1),
                             jnp.concatenate([-s1[:, :hn], c1[:, :hn]], 1)], 0).astype(BF)
    m_filt = jnp.concatenate([c1, -s1], 0).astype(BF)
    m_inv = jnp.concatenate([jnp.concatenate([c1[:hn], -s1[:hn]], 1),
                             jnp.concatenate([s1[:hn], c1[:hn]], 1)], 0).astype(BF)
    a2 = (2.0 * math.pi / n2) * ((i2[:, None] * i2[None, :]) % n2).astype(F32)
    c2, s2 = jnp.cos(a2), jnp.sin(a2)
    g2f = jnp.concatenate([jnp.concatenate([c2, -s2], 1), jnp.concatenate([s2, c2], 1)], 0).astype(BF)
    g2i = jnp.concatenate([jnp.concatenate([c2, s2], 1), jnp.concatenate([-s2, c2], 1)], 0).astype(BF)
    at = (2.0 * math.pi / n) * (i1[:, None] * i2[None, :]).astype(F32)
    return dict(m_sig=m_sig, m_filt=m_filt, m_inv=m_inv, g2f=g2f, g2i=g2i, twr=jnp.cos(at), twi=-jnp.sin(at))


def _pick(total, want):
    b = min(total, want)
    while total % b:
        b //= 2
    return b


def _out_proj_kernel(x_ref, a_ref, hy_ref, ga_ref, gh_ref, wa_ref, wh_ref, o_ref):
    a = a_ref[...].astype(F32)
    ra = lax.rsqrt(jnp.sum(a * a, axis=-1, keepdims=True) * (1.0 / ATTN_W) + EPS)
    an = (a * ra * ga_ref[...]).astype(BF)
    acc = x_ref[...] + _dot(an, wa_ref[...])
    gh = gh_ref[...]
    wh = wh_ref[...]
    for s in range(hy_ref.shape[0]):
        hy = hy_ref[s].astype(F32)
        rh = lax.rsqrt(jnp.sum(hy * hy, axis=0, keepdims=True) * (1.0 / HY_W) + EPS)
        hn = (hy * rh * gh).astype(BF)
        part = lax.dot_general(hn, wh, (((0,), (0,)), ((), ())), preferred_element_type=F32)
        o_ref[s * LANES:(s + 1) * LANES, :] = acc[s * LANES:(s + 1) * LANES, :] + part


def _out_proj(x2d, attn, hy, lw, B, L, tm):
    T = x2d.shape[0]
    tps = L // tm
    ws = [lw['ga'], lw['gh_t'], lw['w_out_a'], lw['w_out_h']]
    return pl.pallas_call(
        _out_proj_kernel,
        out_shape=jax.ShapeDtypeStruct((T, D_MODEL), F32),
        grid=(T // tm,),
        in_specs=[pl.BlockSpec((tm, D_MODEL), lambda i: (i, 0)),
                  pl.BlockSpec((tm, PAD_W), lambda i: (i, 0)),
                  pl.BlockSpec((None, tm // LANES, HY_W, LANES), lambda i: (i // tps, i % tps, 0, 0))]
        + [_full(w) for w in ws],
        out_specs=pl.BlockSpec((tm, D_MODEL), lambda i: (i, 0)),
        compiler_params=_cparams(("parallel",)), name="out_proj",
    )(x2d, attn, hy, *ws)


def _silu_mul(g, u):
    return g * (1.0 / (1.0 + jnp.exp(-g))) * u


def _ffn_kernel(x_ref, g_ref, wg_ref, wu_ref, wd_ref, o_ref, hn_sc, acc_sc):
    j = pl.program_id(1)

    @pl.when(j == 0)
    def _():
        hn_sc[...] = _rms(x_ref[...], g_ref[...]).astype(BF)
        acc_sc[...] = jnp.zeros(acc_sc.shape, F32)

    hn = hn_sc[...]
    a = _silu_mul(_dot(hn, wg_ref[...]), _dot(hn, wu_ref[...])).astype(BF)
    acc_sc[...] += _dot(a, wd_ref[...])

    @pl.when(j == pl.num_programs(1) - 1)
    def _():
        o_ref[...] = x_ref[...] + acc_sc[...]


def _ffn(x2d, g, wg, wu, wd, tm, tf):
    T = x2d.shape[0]
    ff = wg.shape[1]
    return pl.pallas_call(
        _ffn_kernel,
        out_shape=jax.ShapeDtypeStruct((T, D_MODEL), F32),
        grid=(T // tm, ff // tf),
        in_specs=[pl.BlockSpec((tm, D_MODEL), lambda i, j: (i, 0)),
                  pl.BlockSpec((1, D_MODEL), lambda i, j: (0, 0)),
                  pl.BlockSpec((D_MODEL, tf), lambda i, j: (0, j)),
                  pl.BlockSpec((D_MODEL, tf), lambda i, j: (0, j)),
                  pl.BlockSpec((tf, D_MODEL), lambda i, j: (j, 0))],
        out_specs=pl.BlockSpec((tm, D_MODEL), lambda i, j: (i, 0)),
        scratch_shapes=[pltpu.VMEM((tm, D_MODEL), BF), pltpu.VMEM((tm, D_MODEL), F32)],
        compiler_params=_cparams(("parallel", "arbitrary")), name="ffn",
    )(x2d, g, wg, wu, wd)


def _moe_kernel(x_ref, g_ref, rwt_ref, wg_ref, wu_ref, wd_ref, o_ref, hn_sc, sel_sc, pos_sc, gate_sc, acc_sc,
                oh_sc, g_sc, xg_sc, y_sc, *, cap):
    e = pl.program_id(1)
    j = pl.program_id(2)
    tm = x_ref.shape[0]

    @pl.when((e == 0) & (j == 0))
    def _():
        hn = _rms(x_ref[...], g_ref[...])
        hn_sc[...] = hn.astype(BF)
        lg = lax.dot_general(rwt_ref[...], hn, (((1,), (1,)), ((), ())), precision=HI, preferred_element_type=F32)
        row = lax.broadcasted_iota(jnp.int32, lg.shape, 0)
        m1 = jnp.max(lg, axis=0, keepdims=True)
        i1 = jnp.min(jnp.where(lg == m1, row, N_EXPERTS), axis=0, keepdims=True)
        lg2 = jnp.where(row == i1, -jnp.inf, lg)
        m2 = jnp.max(lg2, axis=0, keepdims=True)
        i2 = jnp.min(jnp.where(lg2 == m2, row, N_EXPERTS), axis=0, keepdims=True)
        e2 = jnp.exp(m2 - m1)
        g1 = 1.0 / (1.0 + e2)
        gate_sc[...] = jnp.where(row == i1, g1, 0.0) + jnp.where(row == i2, e2 * g1, 0.0)
        sel = jnp.where((row == i1) | (row == i2), 1.0, 0.0)
        sel_sc[...] = sel
        r = lax.broadcasted_iota(jnp.int32, (tm, tm), 0)
        c = lax.broadcasted_iota(jnp.int32, (tm, tm), 1)
        pos_sc[...] = _dot(sel.astype(BF), jnp.where(r < c, 1.0, 0.0).astype(BF))
        acc_sc[...] = jnp.zeros(acc_sc.shape, F32)

    sel_row = sel_sc[pl.ds(e, 1), :]
    pos_row = pos_sc[pl.ds(e, 1), :]
    gate_row = gate_sc[pl.ds(e, 1), :]
    count = jnp.sum(sel_row).astype(jnp.int32)
    slot = lax.broadcasted_iota(jnp.int32, (cap, tm), 0).astype(F32)
    last_j = pl.num_programs(2) - 1

    def one_hot(c):
        base = jnp.asarray(c * cap).astype(F32)
        onehot = jnp.where((pos_row - base == slot) & (sel_row > 0.0), 1.0, 0.0)
        gate = jnp.sum(onehot * gate_row, axis=1, keepdims=True)
        return onehot.astype(BF), gate

    def expert(xg):
        h = _silu_mul(_dot(xg, wg_ref[...]), _dot(xg, wu_ref[...])).astype(BF)
        return _dot(h, wd_ref[...])

    def scatter(onehot, y):
        acc_sc[...] += lax.dot_general(onehot, y.astype(BF), (((0,), (0,)), ((), ())), preferred_element_type=F32)

    @pl.when(j == 0)
    def _():
        onehot, gate = one_hot(0)
        oh_sc[...] = onehot
        g_sc[...] = gate
        xg_sc[...] = _dot(onehot, hn_sc[...]).astype(BF)
        y_sc[...] = jnp.zeros(y_sc.shape, F32)

    y_sc[...] += expert(xg_sc[...])

    @pl.when(j == last_j)
    def _():
        scatter(oh_sc[...], y_sc[...] * g_sc[...])

    def chunk(c, carry):
        onehot, gate = one_hot(c)
        xg = _dot(onehot, hn_sc[...]).astype(BF)
        scatter(onehot, expert(xg) * gate)
        return carry

    lax.fori_loop(1, (count + cap - 1) // cap, chunk, 0)

    @pl.when((e == pl.num_programs(1) - 1) & (j == last_j))
    def _():
        o_ref[...] = x_ref[...] + acc_sc[...]


def _moe(x2d, g, rwt, wg, wu, wd, tm, tf, cap):
    T = x2d.shape[0]
    ne, _, ff = wg.shape
    return pl.pallas_call(
        functools.partial(_moe_kernel, cap=cap),
        out_shape=jax.ShapeDtypeStruct((T, D_MODEL), F32),
        grid=(T // tm, ne, ff // tf),
        in_specs=[pl.BlockSpec((tm, D_MODEL), lambda i, e, j: (i, 0)),
                  pl.BlockSpec((1, D_MODEL), lambda i, e, j: (0, 0)),
                  pl.BlockSpec((ne, D_MODEL), lambda i, e, j: (0, 0)),
                  pl.BlockSpec((None, D_MODEL, tf), lambda i, e, j: (e, 0, j)),
                  pl.BlockSpec((None, D_MODEL, tf), lambda i, e, j: (e, 0, j)),
                  pl.BlockSpec((None, tf, D_MODEL), lambda i, e, j: (e, j, 0))],
        out_specs=pl.BlockSpec((tm, D_MODEL), lambda i, e, j: (i, 0)),
        scratch_shapes=[pltpu.VMEM((tm, D_MODEL), BF), pltpu.VMEM((ne, tm), F32), pltpu.VMEM((ne, tm), F32),
                        pltpu.VMEM((ne, tm), F32), pltpu.VMEM((tm, D_MODEL), F32),
                        pltpu.VMEM((cap, tm), BF), pltpu.VMEM((cap, 1), F32),
                        pltpu.VMEM((cap, D_MODEL), BF), pltpu.VMEM((cap, D_MODEL), F32)],
        compiler_params=_cparams(("parallel", "arbitrary", "arbitrary")), name="moe",
    )(x2d, g, rwt, wg, wu, wd)


def _pad_heads(w, d):
    lead = w.shape[:-1]
    w = w.reshape(lead + (N_HEADS, d))
    w = jnp.pad(w, [(0, 0)] * len(lead) + [(0, 0), (0, HEAD_PAD - d)])
    return w.reshape(lead + (PAD_W,))


def _per_channel(a):
    return jnp.broadcast_to(a.astype(F32)[..., None, None], a.shape + (1, LANES))


def _layer_weights(p, layer):
    (mix_norm, w_in, cq_norm, ckv_norm, w_uq, w_ukv, q_norm, k_norm, conv_w, conv_b,
     filt_w1, filt_b1, filt_freq, filt_w2, filt_b2, filt_w3, filt_decay, hyena_d,
     out_norm, w_out, ffn_norm) = [a[layer] for a in p]
    na = Q_LORA + KV_LORA + QK_ROPE
    ukv = w_ukv.reshape(KV_LORA, N_HEADS, QK_NOPE + V_DIM)
    eye = jnp.eye(QK_ROPE, dtype=F32)
    p_kr = jnp.pad(eye, ((0, 0), (QK_NOPE, HEAD_PAD - QK_DIM)))
    row = lambda a: a.reshape(1, -1).astype(F32)
    col = lambda a: a.reshape(-1, 1).astype(F32)
    half = QK_ROPE // 2
    uq = w_uq.reshape(Q_LORA, N_HEADS, QK_DIM)
    swap = lambda a: jnp.concatenate([a[..., half:], a[..., :half]], -1)
    swap_gain = lambda g: jnp.concatenate([g[:QK_NOPE], swap(g[QK_NOPE:])])
    gain_row = lambda g: jnp.pad(g, (0, HEAD_PAD - QK_DIM)).reshape(1, HEAD_PAD).astype(F32)
    q_scale = QK_DIM ** -0.5 * math.log2(math.e)
    w_out_a = w_out[:ATTN_W].reshape(N_HEADS, V_DIM, D_MODEL)
    w_out_a = jnp.pad(w_out_a, ((0, 0), (0, HEAD_PAD - V_DIM), (0, 0))).reshape(PAD_W, D_MODEL)
    w3 = filt_w3.reshape(FILT_HID, 2, 2, HY_W).transpose(2, 1, 3, 0).reshape(2, 2 * HY_W, FILT_HID)
    dec = filt_decay.reshape(2, 2, HY_W).transpose(1, 0, 2).reshape(2, 2 * HY_W, 1)
    return dict(
        mix_norm=row(mix_norm), w_a=w_in[:, :na].astype(BF), w_hy_t=w_in[:, na:].T.astype(BF),
        cq_norm=row(cq_norm), ckv_norm=row(ckv_norm),
        w_uq=_pad_heads(w_uq, QK_DIM).astype(BF),
        w_uk=_pad_heads(ukv[:, :, :QK_NOPE].reshape(KV_LORA, -1), QK_NOPE).astype(BF),
        w_uv=_pad_heads(ukv[:, :, QK_NOPE:].reshape(KV_LORA, -1), V_DIM).astype(BF),
        w_uq_sw=_pad_heads(jnp.concatenate([jnp.zeros_like(uq[:, :, :QK_NOPE]), swap(uq[:, :, QK_NOPE:])], -1)
                           .reshape(Q_LORA, -1), QK_DIM).astype(BF),
        p_kr=jnp.tile(p_kr, (1, N_HEADS)).astype(BF),
        p_kr_sw=jnp.tile(jnp.pad(swap(eye), ((0, 0), (QK_NOPE, HEAD_PAD - QK_DIM))), (1, N_HEADS)).astype(BF),
        gq_cos=gain_row(q_norm) * q_scale, gq_sin=gain_row(swap_gain(q_norm)) * q_scale,
        gk_cos=gain_row(k_norm), gk_sin=gain_row(swap_gain(k_norm)),
        conv_w_t=_per_channel(conv_w.reshape(3, 3, HY_W)), conv_b_t=_per_channel(conv_b.reshape(3, HY_W)),
        filt_w1_t=jnp.pad(filt_w1, ((0, LANES - POS_EMB), (0, 0))).T, filt_b1=col(filt_b1),
        filt_freq=col(filt_freq), filt_w2_t=filt_w2.T, filt_b2=col(filt_b2), filt_w3_t=w3, filt_decay_t=dec,
        hyena_d_t=_per_channel(hyena_d),
        ga=_pad_heads(out_norm[:ATTN_W], V_DIM).reshape(1, PAD_W),
        gh_t=jnp.broadcast_to(col(out_norm[ATTN_W:]), (HY_W, LANES)),
        w_out_a=w_out_a.astype(BF), w_out_h=w_out[ATTN_W:].astype(BF),
        ffn_norm=row(ffn_norm),
    )


def _position_tables(L):
    pos = jnp.arange(L, dtype=F32)
    half = QK_ROPE // 2
    inv = ROPE_THETA ** (-jnp.arange(0, QK_ROPE, 2, dtype=F32) / QK_ROPE)
    ang = pos[:, None] * inv[None, :]
    cos, sin = jnp.cos(ang), jnp.sin(ang)
    zeros = lambda w: jnp.zeros((L, w), F32)
    cos_t = jnp.concatenate([jnp.ones((L, QK_NOPE), F32), cos, cos, zeros(HEAD_PAD - QK_DIM)], 1)
    sin_t = jnp.concatenate([zeros(QK_NOPE), -sin, sin, zeros(HEAD_PAD - QK_DIM)], 1)
    n = jnp.arange(2 * L, dtype=jnp.int32)
    p = jnp.where(n < L, n, 2 * L - n)
    p = jnp.where(n == L, 0, p)
    t = jnp.take(jnp.linspace(0.0, 1.0, L, dtype=F32), p)[None, :]
    w = 2.0 * math.pi * p.astype(F32) / L
    f = jnp.linspace(1e-4, N_BANDS - 1, N_BANDS, dtype=F32)
    a = f[:, None] * w[None, :]
    zt = jnp.concatenate([t, jnp.cos(a), -jnp.sin(a), jnp.zeros((LANES - POS_EMB, 2 * L), F32)], axis=0)
    return (cos_t, sin_t), zt


def _moe_capacity(tm):
    return max(16, (tm * 9 // 32) // 16 * 16)


def _trunk(x, layers, mixers, plan, rope, zt):
    B, L, _ = x.shape
    T = B * L
    x2d = x.reshape(T, D_MODEL)
    tm = _pick(L, 512)
    for lw, mixer in zip(layers, mixers):
        q, k, v = _in_proj(x2d, L, lw, rope, tm)
        hv, hx1, hx2 = _hy_proj(x2d, B, L, lw, _pick(L, 2048))
        attn = _attn(q, k, v, B, L, _pick(L, ATTN_TQ))
        taps = _filter_taps(zt, lw, L, _pick(L, 1024))
        hy = _hyena(hv, hx1, hx2, taps, lw, plan, B, L)
        x2d = _out_proj(x2d, attn, hy, lw, B, L, tm)
        tf = 1408
        if mixer[0] == 'dense':
            _, wg, wu, wd = mixer
            x2d = _ffn(x2d, lw['ffn_norm'], wg, wu, wd, _pick(T, 1024), tf)
        else:
            _, rwt, wg, wu, wd = mixer
            tme = _pick(T, 1024)
            x2d = _moe(x2d, lw['ffn_norm'], rwt, wg, wu, wd, tme, tf, _moe_capacity(tme))
    return x2d.reshape(B, L, D_MODEL)


def kernel(x_prompt, x_sample, mix_norm, w_in, cq_norm, ckv_norm, w_uq, w_ukv, q_norm, k_norm, conv_w, conv_b, filt_w1, filt_b1, filt_freq, filt_w2, filt_b2, filt_w3, filt_decay, hyena_d, out_norm, w_out, ffn_norm, dense_wg, dense_wu, dense_wd, router_w, moe_wg, moe_wu, moe_wd):
    per_layer = (mix_norm, w_in, cq_norm, ckv_norm, w_uq, w_ukv, q_norm, k_norm, conv_w, conv_b,
                 filt_w1, filt_b1, filt_freq, filt_w2, filt_b2, filt_w3, filt_decay, hyena_d,
                 out_norm, w_out, ffn_norm)
    depth = mix_norm.shape[0]
    layers = [_layer_weights(per_layer, l) for l in range(depth)]
    mixers = []
    for l in range(depth):
        j = l // 2
        if l % 2 == 0:
            mixers.append(('dense', dense_wg[j].astype(BF), dense_wu[j].astype(BF), dense_wd[j].astype(BF)))
        else:
            mixers.append(('moe', router_w[j].T, moe_wg[j].astype(BF), moe_wu[j].astype(BF), moe_wd[j].astype(BF)))
    outs = []
    for x in (x_prompt, x_sample):
        L = x.shape[1]
        rope, zt = _position_tables(L)
        outs.append(_trunk(x, layers, mixers, _fft_plan(L), rope, zt))
    return tuple(outs)
```

```python
import functools
import math

import jax
import jax.numpy as jnp
from jax import lax
from jax.experimental import pallas as pl
from jax.experimental.pallas import tpu as pltpu

BF = jnp.bfloat16
F32 = jnp.float32

D_MODEL = 1024
N_HEADS = 8
QK_NOPE = 64
QK_ROPE = 32
QK_DIM = QK_NOPE + QK_ROPE
V_DIM = 64
Q_LORA = 256
KV_LORA = 128
ATTN_W = N_HEADS * V_DIM
HY_W = D_MODEL - ATTN_W
ROPE_THETA = 10000.0
N_BANDS = 16
POS_EMB = 1 + 2 * N_BANDS
FILT_HID = 64
N_EXPERTS = 8
EPS = 1e-6

LANES = 128
HEAD_PAD = LANES
PAD_W = N_HEADS * HEAD_PAD
VMEM_LIMIT = 56 * 1024 * 1024
HI = lax.Precision.HIGHEST
ATTN_TQ = 256
ATTN_KV_BYTES = 8 * 1024 * 1024
MOE_TM = 1024
MOE_SUBTILES = 2
HY_CB = 16
HY_DT = F32


def _cparams(sem):
    return pltpu.CompilerParams(dimension_semantics=sem, vmem_limit_bytes=VMEM_LIMIT)


def _rms(x, g):
    ms = jnp.mean(x * x, axis=-1, keepdims=True)
    return x * lax.rsqrt(ms + EPS) * g


def _dot(a, b):
    return jnp.dot(a, b, preferred_element_type=F32)


def _full(a):
    return pl.BlockSpec(a.shape, lambda *_: (0,) * a.ndim)


def _in_proj_kernel(x_ref, gmix_ref, wa_ref, cqn_ref, ckvn_ref, wuq_ref, wuqs_ref, wuk_ref, wuv_ref, pkr_ref,
                    pkrs_ref, cq_ref, sq_ref, ck_ref, sk_ref, q_ref, k_ref, v_ref):
    h = _rms(x_ref[...], gmix_ref[...]).astype(BF)
    pa = _dot(h, wa_ref[...])
    c_q = pa[:, 0:Q_LORA]
    c_kv = pa[:, Q_LORA:Q_LORA + KV_LORA]
    k_r = pa[:, Q_LORA + KV_LORA:Q_LORA + KV_LORA + QK_ROPE].astype(BF)
    cqn = _rms(c_q, cqn_ref[...]).astype(BF)
    ckvn = _rms(c_kv, ckvn_ref[...]).astype(BF)
    q = _dot(cqn, wuq_ref[...])
    k = _dot(ckvn, wuk_ref[...]) + _dot(k_r, pkr_ref[...])
    q_sw = _dot(cqn, wuqs_ref[...])
    k_sw = _dot(k_r, pkrs_ref[...])
    lane = lax.broadcasted_iota(jnp.int32, (1, PAD_W), 1)
    ones_col = jnp.where(lane % HEAD_PAD == V_DIM, 1.0, 0.0)
    v_ref[...] = (_dot(ckvn, wuv_ref[...]) + ones_col).astype(BF)
    for src, swp, cref, sref, oref in ((q, q_sw, cq_ref, sq_ref, q_ref), (k, k_sw, ck_ref, sk_ref, k_ref)):
        ct = cref[...]
        st = sref[...]
        for hh in range(N_HEADS):
            sl = slice(HEAD_PAD * hh, HEAD_PAD * (hh + 1))
            t = src[:, sl]
            ms = jnp.sum(t * t, axis=-1, keepdims=True) * (1.0 / QK_DIM)
            oref[:, sl] = ((t * ct + swp[:, sl] * st) * lax.rsqrt(ms + EPS)).astype(BF)


def _in_proj(x2d, L, lw, rope, tm):
    T = x2d.shape[0]
    tps = L // tm
    weights = [lw['mix_norm'], lw['w_a'], lw['cq_norm'], lw['ckv_norm'],
               lw['w_uq'], lw['w_uq_sw'], lw['w_uk'], lw['w_uv'], lw['p_kr'], lw['p_kr_sw']]
    cos_t, sin_t = rope
    tables = [cos_t * lw['gq_cos'], sin_t * lw['gq_sin'], cos_t * lw['gk_cos'], sin_t * lw['gk_sin']]
    in_specs = ([pl.BlockSpec((tm, D_MODEL), lambda i: (i, 0))] + [_full(w) for w in weights]
                + [pl.BlockSpec((tm, LANES), lambda i: (i % tps, 0))] * 4)
    return pl.pallas_call(
        _in_proj_kernel,
        out_shape=[jax.ShapeDtypeStruct((T, PAD_W), BF)] * 3, grid=(T // tm,), in_specs=in_specs,
        out_specs=[pl.BlockSpec((tm, PAD_W), lambda i: (i, 0))] * 3,
        compiler_params=_cparams(("parallel",)), name="in_proj",
    )(x2d, *weights, *tables)


def _hy_proj_kernel(x_ref, gmix_ref, wt_ref, v_ref, x1_ref, x2_ref):
    h = _rms(x_ref[...], gmix_ref[...]).astype(BF)
    for g, oref in enumerate((v_ref, x1_ref, x2_ref)):
        ut = lax.dot_general(wt_ref[g * HY_W:(g + 1) * HY_W, :], h, (((1,), (1,)), ((), ())),
                             preferred_element_type=F32)
        for a in range(oref.shape[1]):
            oref[:, a, :] = ut[:, a * LANES:(a + 1) * LANES].astype(oref.dtype)


def _hy_proj(x2d, B, L, lw, tm):
    tps = L // tm
    rows = tm // LANES
    shape = jax.ShapeDtypeStruct((B, HY_W, L // LANES, LANES), HY_DT)
    spec = pl.BlockSpec((None, HY_W, rows, LANES), lambda i: (i // tps, 0, i % tps, 0))
    return pl.pallas_call(
        _hy_proj_kernel, out_shape=[shape] * 3, grid=(B * tps,),
        in_specs=[pl.BlockSpec((tm, D_MODEL), lambda i: (i, 0)), _full(lw['mix_norm']), _full(lw['w_hy_t'])],
        out_specs=[spec] * 3,
        compiler_params=_cparams(("parallel",)), name="hy_proj",
    )(x2d, lw['mix_norm'], lw['w_hy_t'])


def _attn_kernel(q_ref, k_ref, v_ref, o_ref, *, heads):
    lane = lax.broadcasted_iota(jnp.int32, (1, HEAD_PAD), 1)
    for h in range(heads):
        sl = slice(HEAD_PAD * h, HEAD_PAD * (h + 1))
        s = lax.dot_general(q_ref[:, sl], k_ref[:, sl], (((1,), (1,)), ((), ())), preferred_element_type=F32)
        m = jnp.max(s, axis=-1, keepdims=True)
        p = jnp.exp2(s - m).astype(BF)
        a = _dot(p, v_ref[:, sl])
        o = jnp.where(lane < V_DIM, a / a[:, V_DIM:V_DIM + 1], 0.0)
        o_ref[:, sl] = o.astype(o_ref.dtype)


def _attn(q, k, v, B, L, tq):
    heads = max(h for h in (1, 2, 4) if h == 1 or L * h * HEAD_PAD * 2 <= ATTN_KV_BYTES)
    kv_mode = dict(pipeline_mode=pl.Buffered(1)) if L * heads * HEAD_PAD * 2 > ATTN_KV_BYTES // 2 else {}
    nq = L // tq
    w = heads * HEAD_PAD
    return pl.pallas_call(
        functools.partial(_attn_kernel, heads=heads),
        out_shape=jax.ShapeDtypeStruct(q.shape, BF),
        grid=(B, N_HEADS // heads, nq),
        in_specs=[
            pl.BlockSpec((tq, w), lambda b, h, qi: (b * nq + qi, h)),
            pl.BlockSpec((L, w), lambda b, h, qi: (b, h), **kv_mode),
            pl.BlockSpec((L, w), lambda b, h, qi: (b, h), **kv_mode),
        ],
        out_specs=pl.BlockSpec((tq, w), lambda b, h, qi: (b * nq + qi, h)),
        compiler_params=_cparams(("parallel", "parallel", "parallel")), name="attn",
    )(q, k, v)


def _filter_kernel(z_ref, w1_ref, b1_ref, fr_ref, w2_ref, b2_ref, w3_ref, dec_ref, o_ref, *, seq_len):
    z = z_ref[...]
    fr = fr_ref[...]
    h = jnp.sin(fr * (jnp.dot(w1_ref[...], z, precision=HI, preferred_element_type=F32) + b1_ref[...]))
    h = jnp.sin(fr * (jnp.dot(w2_ref[...], h, precision=HI, preferred_element_type=F32) + b2_ref[...]))
    t = z[0:1, :]
    o = jnp.dot(w3_ref[...], h, precision=HI, preferred_element_type=F32) * jnp.exp(-t * jnp.abs(dec_ref[...]))
    n = pl.program_id(0) * z.shape[1] + lax.broadcasted_iota(jnp.int32, (1, z.shape[1]), 1)
    o = jnp.where(n == seq_len, 0.0, o)
    for a in range(o_ref.shape[1]):
        o_ref[:, a, :] = o[:, a * LANES:(a + 1) * LANES]


def _filter_taps(zt, lw, L, tl):
    n = 2 * L
    ch = lw['filt_w3_t'].shape[1]
    half_tiles = L // tl
    sel = lambda i: (i // half_tiles, 0, 0)
    ws = [lw['filt_w1_t'], lw['filt_b1'], lw['filt_freq'], lw['filt_w2_t'], lw['filt_b2']]
    return pl.pallas_call(
        functools.partial(_filter_kernel, seq_len=L),
        out_shape=jax.ShapeDtypeStruct((ch, n // LANES, LANES), F32),
        grid=(n // tl,),
        in_specs=[pl.BlockSpec((LANES, tl), lambda i: (0, i))] + [_full(w) for w in ws]
        + [pl.BlockSpec((None, ch, FILT_HID), sel), pl.BlockSpec((None, ch, 1), sel)],
        out_specs=pl.BlockSpec((ch, tl // LANES, LANES), lambda i: (0, i, 0)),
        compiler_params=_cparams(("parallel",)), name="filt_taps",
    )(zt, *ws, lw['filt_w3_t'], lw['filt_decay_t'])


def _lanes_of(parts):
    return parts[0] if len(parts) == 1 else jnp.concatenate(parts, axis=1)


def _hyena_kernel(v_ref, x1_ref, x2_ref, kt_ref, cw_ref, cb_ref, d_ref, msig_ref, mfilt_ref, minv_ref,
                  g2f_ref, g2i_ref, twr_ref, twi_ref, o_ref, kf_sc, *, scale):
    _, cb, nh, _ = v_ref.shape
    n1 = 2 * nh
    twr = twr_ref[...]
    twi = twi_ref[...]

    def fwd(s_all, m):
        a = _dot(m, s_all)
        rows = []
        for c in range(cb):
            ar = a[:n1, c * LANES:(c + 1) * LANES]
            ai = a[n1:, c * LANES:(c + 1) * LANES]
            rows.append(jnp.concatenate([ar * twr - ai * twi, ar * twi + ai * twr], axis=1).astype(BF))
        return _dot(jnp.concatenate(rows, axis=0), g2f_ref[...])

    def inv(z):
        b = _dot(z.astype(BF), g2i_ref[...])
        cols = []
        for c in range(cb):
            br = b[c * n1:(c + 1) * n1, :LANES]
            bi = b[c * n1:(c + 1) * n1, LANES:]
            cols.append(jnp.concatenate([br * twr + bi * twi, bi * twr - br * twi], axis=0).astype(BF))
        return _dot(minv_ref[...], _lanes_of(cols))

    def cmul(x, kf):
        xr, xi = x[:, :LANES], x[:, LANES:]
        kr, ki = kf[:, :LANES], kf[:, LANES:]
        return jnp.concatenate([xr * kr - xi * ki, xr * ki + xi * kr], axis=1)

    @pl.when(pl.program_id(1) == 0)
    def _():
        for o in range(2):
            taps = _lanes_of([kt_ref[o, c] for c in range(cb)]).astype(BF)
            kf_sc[o] = fwd(taps, mfilt_ref[...]) * scale

    row = lax.broadcasted_iota(jnp.int32, (cb * nh, LANES), 0) % nh
    lane = lax.broadcasted_iota(jnp.int32, (cb * nh, LANES), 1)

    def short_conv(ref, g):
        w = [jnp.broadcast_to(cw_ref[k, g], (cb, nh, LANES)).reshape(cb * nh, LANES) for k in range(3)]
        bias = jnp.broadcast_to(cb_ref[g], (cb, nh, LANES)).reshape(cb * nh, LANES)
        halves = []
        for r in range(2):
            u = ref[r].astype(F32).reshape(cb * nh, LANES)
            lp = pltpu.roll(u, 1, 1)
            prev = jnp.where(lane == 0, pltpu.roll(lp, 1, 0), lp)
            prev = jnp.where((lane == 0) & (row == 0), 0.0, prev)
            ln = pltpu.roll(u, LANES - 1, 1)
            nxt = jnp.where(lane == LANES - 1, pltpu.roll(ln, cb * nh - 1, 0), ln)
            nxt = jnp.where((lane == LANES - 1) & (row == nh - 1), 0.0, nxt)
            y = prev * w[0] + u * w[1] + nxt * w[2] + bias
            halves.append(_lanes_of([y[c * nh:(c + 1) * nh] for c in range(cb)]))
        return jnp.concatenate(halves, axis=0)

    def skip(o):
        return _lanes_of([d_ref[o, c] for c in range(cb)])

    v = short_conv(v_ref, 0)
    x1 = short_conv(x1_ref, 1)
    x2 = short_conv(x2_ref, 2)
    y = inv(cmul(fwd(v.astype(BF), msig_ref[...]), kf_sc[0]))
    z1 = (x1 * (y + skip(0) * v)).astype(BF)
    y = inv(cmul(fwd(z1, msig_ref[...]), kf_sc[1]))
    hy = x2 * (y + skip(1) * z1.astype(F32))
    for c in range(cb):
        o_ref[0, :, c, :] = hy[:nh, c * LANES:(c + 1) * LANES].astype(o_ref.dtype)
        o_ref[1, :, c, :] = hy[nh:, c * LANES:(c + 1) * LANES].astype(o_ref.dtype)


def _hyena(hv, hx1, hx2, taps, lw, plan, B, L):
    nh = L // LANES
    n1 = 2 * nh
    P = B // 2
    cb = HY_CB
    pair = lambda a: a.reshape(P, 2, HY_W, nh, LANES)
    sig = pl.BlockSpec((None, 2, cb, nh, LANES), lambda j, p: (p, 0, j, 0, 0))
    consts = [plan['m_sig'], plan['m_filt'], plan['m_inv'], plan['g2f'], plan['g2i'], plan['twr'], plan['twi']]
    out = pl.pallas_call(
        functools.partial(_hyena_kernel, scale=1.0 / (2 * L)),
        out_shape=jax.ShapeDtypeStruct((P, 2, nh, HY_W, LANES), HY_DT),
        grid=(HY_W // cb, P),
        in_specs=[sig, sig, sig,
                  pl.BlockSpec((2, cb, n1, LANES), lambda j, p: (0, j, 0, 0)),
                  pl.BlockSpec((3, 3, cb, 1, LANES), lambda j, p: (0, 0, j, 0, 0)),
                  pl.BlockSpec((3, cb, 1, LANES), lambda j, p: (0, j, 0, 0)),
                  pl.BlockSpec((2, cb, 1, LANES), lambda j, p: (0, j, 0, 0))] + [_full(c) for c in consts],
        out_specs=pl.BlockSpec((None, 2, nh, cb, LANES), lambda j, p: (p, 0, 0, j, 0)),
        scratch_shapes=[pltpu.VMEM((2, cb * n1, 2 * LANES), F32)],
        compiler_params=_cparams(("parallel", "arbitrary")), name="hyena",
    )(pair(hv), pair(hx1), pair(hx2), taps.reshape(2, HY_W, n1, LANES), lw['conv_w_t'], lw['conv_b_t'],
      lw['hyena_d_t'], *consts)
    return out.reshape(B, nh, HY_W, LANES)


def _fft_plan(L):
    n = 2 * L
    n2 = LANES
    n1 = n // n2
    i1 = jnp.arange(n1, dtype=jnp.int32)
    i2 = jnp.arange(n2, dtype=jnp.int32)
    a1 = (2.0 * math.pi / n1) * ((i1[:, None] * i1[None, :]) % n1).astype(F32)
    c1, s1 = jnp.cos(a1), jnp.sin(a1)
    hn = n1 // 2
    m_sig = jnp.concatenate([jnp.concatenate([c1[:, :hn], s1[:, :hn]], 1),
                             jnp.concatenate([-s1[:, :hn], c1[:, :hn]], 1)], 0).astype(BF)
    m_filt = jnp.concatenate([c1, -s1], 0).astype(BF)
    m_inv = jnp.concatenate([jnp.concatenate([c1[:hn], -s1[:hn]], 1),
                             jnp.concatenate([s1[:hn], c1[:hn]], 1)], 0).astype(BF)
    a2 = (2.0 * math.pi / n2) * ((i2[:, None] * i2[None, :]) % n2).astype(F32)
    c2, s2 = jnp.cos(a2), jnp.sin(a2)
    g2f = jnp.concatenate([jnp.concatenate([c2, -s2], 1), jnp.concatenate([s2, c2], 1)], 0).astype(BF)
    g2i = jnp.concatenate([jnp.concatenate([c2, s2], 1), jnp.concatenate([-s2, c2], 1)], 0).astype(BF)
    at = (2.0 * math.pi / n) * (i1[:, None] * i2[None, :]).astype(F32)
    return dict(m_sig=m_sig, m_filt=m_filt, m_inv=m_inv, g2f=g2f, g2i=g2i, twr=jnp.cos(at), twi=-jnp.sin(at))


def _pick(total, want):
    b = min(total, want)
    while total % b:
        b //= 2
    return b


def _out_proj_kernel(x_ref, a_ref, hy_ref, ga_ref, gh_ref, wa_ref, wh_ref, o_ref):
    a = a_ref[...].astype(F32)
    ra = lax.rsqrt(jnp.sum(a * a, axis=-1, keepdims=True) * (1.0 / ATTN_W) + EPS)
    an = (a * ra * ga_ref[...]).astype(BF)
    acc = x_ref[...] + _dot(an, wa_ref[...])
    gh = gh_ref[...]
    wh = wh_ref[...]
    for s in range(hy_ref.shape[0]):
        hy = hy_ref[s].astype(F32)
        rh = lax.rsqrt(jnp.sum(hy * hy, axis=0, keepdims=True) * (1.0 / HY_W) + EPS)
        hn = (hy * rh * gh).astype(BF)
        part = lax.dot_general(hn, wh, (((0,), (0,)), ((), ())), preferred_element_type=F32)
        o_ref[s * LANES:(s + 1) * LANES, :] = acc[s * LANES:(s + 1) * LANES, :] + part


def _out_proj(x2d, attn, hy, lw, B, L, tm):
    T = x2d.shape[0]
    tps = L // tm
    ws = [lw['ga'], lw['gh_t'], lw['w_out_a'], lw['w_out_h']]
    return pl.pallas_call(
        _out_proj_kernel,
        out_shape=jax.ShapeDtypeStruct((T, D_MODEL), F32),
        grid=(T // tm,),
        in_specs=[pl.BlockSpec((tm, D_MODEL), lambda i: (i, 0)),
                  pl.BlockSpec((tm, PAD_W), lambda i: (i, 0)),
                  pl.BlockSpec((None, tm // LANES, HY_W, LANES), lambda i: (i // tps, i % tps, 0, 0))]
        + [_full(w) for w in ws],
        out_specs=pl.BlockSpec((tm, D_MODEL), lambda i: (i, 0)),
        compiler_params=_cparams(("parallel",)), name="out_proj",
    )(x2d, attn, hy, *ws)


def _silu_mul(g, u):
    return g * (1.0 / (1.0 + jnp.exp(-g))) * u


def _ffn_kernel(x_ref, g_ref, wg_ref, wu_ref, wd_ref, o_ref, hn_sc, acc_sc):
    j = pl.program_id(1)

    @pl.when(j == 0)
    def _():
        hn_sc[...] = _rms(x_ref[...], g_ref[...]).astype(BF)
        acc_sc[...] = jnp.zeros(acc_sc.shape, F32)

    hn = hn_sc[...]
    a = _silu_mul(_dot(hn, wg_ref[...]), _dot(hn, wu_ref[...])).astype(BF)
    acc_sc[...] += _dot(a, wd_ref[...])

    @pl.when(j == pl.num_programs(1) - 1)
    def _():
        o_ref[...] = x_ref[...] + acc_sc[...]


def _ffn(x2d, g, wg, wu, wd, tm, tf):
    T = x2d.shape[0]
    ff = wg.shape[1]
    return pl.pallas_call(
        _ffn_kernel,
        out_shape=jax.ShapeDtypeStruct((T, D_MODEL), F32),
        grid=(T // tm, ff // tf),
        in_specs=[pl.BlockSpec((tm, D_MODEL), lambda i, j: (i, 0)),
                  pl.BlockSpec((1, D_MODEL), lambda i, j: (0, 0)),
                  pl.BlockSpec((D_MODEL, tf), lambda i, j: (0, j)),
                  pl.BlockSpec((D_MODEL, tf), lambda i, j: (0, j)),
                  pl.BlockSpec((tf, D_MODEL), lambda i, j: (j, 0))],
        out_specs=pl.BlockSpec((tm, D_MODEL), lambda i, j: (i, 0)),
        scratch_shapes=[pltpu.VMEM((tm, D_MODEL), BF), pltpu.VMEM((tm, D_MODEL), F32)],
        compiler_params=_cparams(("parallel", "arbitrary")), name="ffn",
    )(x2d, g, wg, wu, wd)


def _moe_kernel(x_ref, g_ref, rwt_ref, wg_ref, wu_ref, wd_ref, o_ref, hn_sc, sel_sc, pos_sc, gate_sc,
                oh_sc, g_sc, xg_sc, y_sc, *, cap):
    e = pl.program_id(1)
    j = pl.program_id(2)
    nsub, tm = sel_sc.shape[0], sel_sc.shape[2]
    last_j = pl.num_programs(2) - 1
    slot = lax.broadcasted_iota(jnp.int32, (cap, tm), 0).astype(F32)

    def expert(xg):
        h = _silu_mul(_dot(xg, wg_ref[...]), _dot(xg, wu_ref[...])).astype(BF)
        return _dot(h, wd_ref[...])

    for s in range(nsub):
        rows = slice(s * tm, (s + 1) * tm)

        @pl.when((e == 0) & (j == 0))
        def _():
            x = x_ref[rows, :]
            hn = _rms(x, g_ref[...])
            hn_sc[s] = hn.astype(BF)
            o_ref[rows, :] = x
            lg = lax.dot_general(rwt_ref[...], hn, (((1,), (1,)), ((), ())), precision=HI,
                                 preferred_element_type=F32)
            row = lax.broadcasted_iota(jnp.int32, lg.shape, 0)
            m1 = jnp.max(lg, axis=0, keepdims=True)
            i1 = jnp.min(jnp.where(lg == m1, row, N_EXPERTS), axis=0, keepdims=True)
            lg2 = jnp.where(row == i1, -jnp.inf, lg)
            m2 = jnp.max(lg2, axis=0, keepdims=True)
            i2 = jnp.min(jnp.where(lg2 == m2, row, N_EXPERTS), axis=0, keepdims=True)
            e2 = jnp.exp(m2 - m1)
            g1 = 1.0 / (1.0 + e2)
            gate_sc[s] = jnp.where(row == i1, g1, 0.0) + jnp.where(row == i2, e2 * g1, 0.0)
            sel = jnp.where((row == i1) | (row == i2), 1.0, 0.0)
            sel_sc[s] = sel
            r = lax.broadcasted_iota(jnp.int32, (tm, tm), 0)
            c = lax.broadcasted_iota(jnp.int32, (tm, tm), 1)
            pos_sc[s] = _dot(sel.astype(BF), jnp.where(r < c, 1.0, 0.0).astype(BF))

        sel_row = sel_sc[s, pl.ds(e, 1), :]
        pos_row = pos_sc[s, pl.ds(e, 1), :]
        gate_row = gate_sc[s, pl.ds(e, 1), :]
        count = jnp.sum(sel_row).astype(jnp.int32)

        def one_hot(c, sel_row=sel_row, pos_row=pos_row, gate_row=gate_row):
            base = jnp.asarray(c * cap).astype(F32)
            onehot = jnp.where((pos_row - base == slot) & (sel_row > 0.0), 1.0, 0.0)
            gate = jnp.sum(onehot * gate_row, axis=1, keepdims=True)
            return onehot.astype(BF), gate

        def scatter(onehot, y, rows=rows):
            o_ref[rows, :] += lax.dot_general(onehot, y.astype(BF), (((0,), (0,)), ((), ())),
                                              preferred_element_type=F32)

        @pl.when(j == 0)
        def _():
            onehot, gate = one_hot(0)
            oh_sc[s] = onehot
            g_sc[s] = gate
            xg_sc[s] = _dot(onehot, hn_sc[s]).astype(BF)
            y_sc[s] = jnp.zeros(y_sc.shape[1:], F32)

        y_sc[s] += expert(xg_sc[s])

        @pl.when(j == last_j)
        def _():
            scatter(oh_sc[s], y_sc[s] * g_sc[s])

        def chunk(c, carry, one_hot=one_hot, scatter=scatter):
            onehot, gate = one_hot(c)
            xg = _dot(onehot, hn_sc[s]).astype(BF)
            scatter(onehot, expert(xg) * gate)
            return carry

        lax.fori_loop(1, (count + cap - 1) // cap, chunk, 0)


def _moe(x2d, g, rwt, wg, wu, wd, tm, nsub, tf, cap):
    T = x2d.shape[0]
    ne, _, ff = wg.shape
    tb = tm * nsub
    return pl.pallas_call(
        functools.partial(_moe_kernel, cap=cap),
        out_shape=jax.ShapeDtypeStruct((T, D_MODEL), F32),
        grid=(T // tb, ne, ff // tf),
        in_specs=[pl.BlockSpec((tb, D_MODEL), lambda i, e, j: (i, 0), pipeline_mode=pl.Buffered(1)),
                  pl.BlockSpec((1, D_MODEL), lambda i, e, j: (0, 0)),
                  pl.BlockSpec((ne, D_MODEL), lambda i, e, j: (0, 0)),
                  pl.BlockSpec((None, D_MODEL, tf), lambda i, e, j: (e, 0, j)),
                  pl.BlockSpec((None, D_MODEL, tf), lambda i, e, j: (e, 0, j)),
                  pl.BlockSpec((None, tf, D_MODEL), lambda i, e, j: (e, j, 0))],
        out_specs=pl.BlockSpec((tb, D_MODEL), lambda i, e, j: (i, 0)),
        scratch_shapes=[pltpu.VMEM((nsub, tm, D_MODEL), BF), pltpu.VMEM((nsub, ne, tm), F32),
                        pltpu.VMEM((nsub, ne, tm), F32), pltpu.VMEM((nsub, ne, tm), F32),
                        pltpu.VMEM((nsub, cap, tm), BF), pltpu.VMEM((nsub, cap, 1), F32),
                        pltpu.VMEM((nsub, cap, D_MODEL), BF), pltpu.VMEM((nsub, cap, D_MODEL), F32)],
        compiler_params=_cparams(("parallel", "arbitrary", "arbitrary")), name="moe",
    )(x2d, g, rwt, wg, wu, wd)


def _pad_heads(w, d):
    lead = w.shape[:-1]
    w = w.reshape(lead + (N_HEADS, d))
    w = jnp.pad(w, [(0, 0)] * len(lead) + [(0, 0), (0, HEAD_PAD - d)])
    return w.reshape(lead + (PAD_W,))


def _per_channel(a):
    return jnp.broadcast_to(a.astype(F32)[..., None, None], a.shape + (1, LANES))


def _layer_weights(p, layer):
    (mix_norm, w_in, cq_norm, ckv_norm, w_uq, w_ukv, q_norm, k_norm, conv_w, conv_b,
     filt_w1, filt_b1, filt_freq, filt_w2, filt_b2, filt_w3, filt_decay, hyena_d,
     out_norm, w_out, ffn_norm) = [a[layer] for a in p]
    na = Q_LORA + KV_LORA + QK_ROPE
    ukv = w_ukv.reshape(KV_LORA, N_HEADS, QK_NOPE + V_DIM)
    eye = jnp.eye(QK_ROPE, dtype=F32)
    p_kr = jnp.pad(eye, ((0, 0), (QK_NOPE, HEAD_PAD - QK_DIM)))
    row = lambda a: a.reshape(1, -1).astype(F32)
    col = lambda a: a.reshape(-1, 1).astype(F32)
    half = QK_ROPE // 2
    uq = w_uq.reshape(Q_LORA, N_HEADS, QK_DIM)
    swap = lambda a: jnp.concatenate([a[..., half:], a[..., :half]], -1)
    swap_gain = lambda g: jnp.concatenate([g[:QK_NOPE], swap(g[QK_NOPE:])])
    gain_row = lambda g: jnp.pad(g, (0, HEAD_PAD - QK_DIM)).reshape(1, HEAD_PAD).astype(F32)
    q_scale = QK_DIM ** -0.5 * math.log2(math.e)
    w_out_a = w_out[:ATTN_W].reshape(N_HEADS, V_DIM, D_MODEL)
    w_out_a = jnp.pad(w_out_a, ((0, 0), (0, HEAD_PAD - V_DIM), (0, 0))).reshape(PAD_W, D_MODEL)
    w3 = filt_w3.reshape(FILT_HID, 2, 2, HY_W).transpose(2, 1, 3, 0).reshape(2, 2 * HY_W, FILT_HID)
    dec = filt_decay.reshape(2, 2, HY_W).transpose(1, 0, 2).reshape(2, 2 * HY_W, 1)
    return dict(
        mix_norm=row(mix_norm), w_a=w_in[:, :na].astype(BF), w_hy_t=w_in[:, na:].T.astype(BF),
        cq_norm=row(cq_norm), ckv_norm=row(ckv_norm),
        w_uq=_pad_heads(w_uq, QK_DIM).astype(BF),
        w_uk=_pad_heads(ukv[:, :, :QK_NOPE].reshape(KV_LORA, -1), QK_NOPE).astype(BF),
        w_uv=_pad_heads(ukv[:, :, QK_NOPE:].reshape(KV_LORA, -1), V_DIM).astype(BF),
        w_uq_sw=_pad_heads(jnp.concatenate([jnp.zeros_like(uq[:, :, :QK_NOPE]), swap(uq[:, :, QK_NOPE:])], -1)
                           .reshape(Q_LORA, -1), QK_DIM).astype(BF),
        p_kr=jnp.tile(p_kr, (1, N_HEADS)).astype(BF),
        p_kr_sw=jnp.tile(jnp.pad(swap(eye), ((0, 0), (QK_NOPE, HEAD_PAD - QK_DIM))), (1, N_HEADS)).astype(BF),
        gq_cos=gain_row(q_norm) * q_scale, gq_sin=gain_row(swap_gain(q_norm)) * q_scale,
        gk_cos=gain_row(k_norm), gk_sin=gain_row(swap_gain(k_norm)),
        conv_w_t=_per_channel(conv_w.reshape(3, 3, HY_W)), conv_b_t=_per_channel(conv_b.reshape(3, HY_W)),
        filt_w1_t=jnp.pad(filt_w1, ((0, LANES - POS_EMB), (0, 0))).T, filt_b1=col(filt_b1),
        filt_freq=col(filt_freq), filt_w2_t=filt_w2.T, filt_b2=col(filt_b2), filt_w3_t=w3, filt_decay_t=dec,
        hyena_d_t=_per_channel(hyena_d),
        ga=_pad_heads(out_norm[:ATTN_W], V_DIM).reshape(1, PAD_W),
        gh_t=jnp.broadcast_to(col(out_norm[ATTN_W:]), (HY_W, LANES)),
        w_out_a=w_out_a.astype(BF), w_out_h=w_out[ATTN_W:].astype(BF),
        ffn_norm=row(ffn_norm),
    )


def _position_tables(L):
    pos = jnp.arange(L, dtype=F32)
    half = QK_ROPE // 2
    inv = ROPE_THETA ** (-jnp.arange(0, QK_ROPE, 2, dtype=F32) / QK_ROPE)
    ang = pos[:, None] * inv[None, :]
    cos, sin = jnp.cos(ang), jnp.sin(ang)
    zeros = lambda w: jnp.zeros((L, w), F32)
    cos_t = jnp.concatenate([jnp.ones((L, QK_NOPE), F32), cos, cos, zeros(HEAD_PAD - QK_DIM)], 1)
    sin_t = jnp.concatenate([zeros(QK_NOPE), -sin, sin, zeros(HEAD_PAD - QK_DIM)], 1)
    n = jnp.arange(2 * L, dtype=jnp.int32)
    p = jnp.where(n < L, n, 2 * L - n)
    p = jnp.where(n == L, 0, p)
    t = jnp.take(jnp.linspace(0.0, 1.0, L, dtype=F32), p)[None, :]
    w = 2.0 * math.pi * p.astype(F32) / L
    f = jnp.linspace(1e-4, N_BANDS - 1, N_BANDS, dtype=F32)
    a = f[:, None] * w[None, :]
    zt = jnp.concatenate([t, jnp.cos(a), -jnp.sin(a), jnp.zeros((LANES - POS_EMB, 2 * L), F32)], axis=0)
    return (cos_t, sin_t), zt


def _moe_capacity(tm):
    return max(16, (tm * 9 // 32) // 16 * 16)


def _trunk(x, layers, mixers, plan, rope, zt):
    B, L, _ = x.shape
    T = B * L
    x2d = x.reshape(T, D_MODEL)
    tm = _pick(L, 512)
    for lw, mixer in zip(layers, mixers):
        q, k, v = _in_proj(x2d, L, lw, rope, tm)
        hv, hx1, hx2 = _hy_proj(x2d, B, L, lw, _pick(L, 2048))
        attn = _attn(q, k, v, B, L, _pick(L, ATTN_TQ))
        taps = _filter_taps(zt, lw, L, _pick(L, 1024))
        hy = _hyena(hv, hx1, hx2, taps, lw, plan, B, L)
        x2d = _out_proj(x2d, attn, hy, lw, B, L, tm)
        tf = 1408
        if mixer[0] == 'dense':
            _, wg, wu, wd = mixer
            x2d = _ffn(x2d, lw['ffn_norm'], wg, wu, wd, _pick(T, 1024), tf)
        else:
            _, rwt, wg, wu, wd = mixer
            tme = _pick(T, MOE_TM)
            x2d = _moe(x2d, lw['ffn_norm'], rwt, wg, wu, wd, tme, _pick(T // tme, MOE_SUBTILES), tf,
                       _moe_capacity(tme))
    return x2d.reshape(B, L, D_MODEL)


def kernel(x_prompt, x_sample, mix_norm, w_in, cq_norm, ckv_norm, w_uq, w_ukv, q_norm, k_norm, conv_w, conv_b, filt_w1, filt_b1, filt_freq, filt_w2, filt_b2, filt_w3, filt_decay, hyena_d, out_norm, w_out, ffn_norm, dense_wg, dense_wu, dense_wd, router_w, moe_wg, moe_wu, moe_wd):
    per_layer = (mix_norm, w_in, cq_norm, ckv_norm, w_uq, w_ukv, q_norm, k_norm, conv_w, conv_b,
                 filt_w1, filt_b1, filt_freq, filt_w2, filt_b2, filt_w3, filt_decay, hyena_d,
                 out_norm, w_out, ffn_norm)
    depth = mix_norm.shape[0]
    layers = [_layer_weights(per_layer, l) for l in range(depth)]
    mixers = []
    for l in range(depth):
        j = l // 2
        if l % 2 == 0:
            mixers.append(('dense', dense_wg[j].astype(BF), dense_wu[j].astype(BF), dense_wd[j].astype(BF)))
        else:
            mixers.append(('moe', router_w[j].T, moe_wg[j].astype(BF), moe_wu[j].astype(BF), moe_wd[j].astype(BF)))
    outs = []
    for x in (x_prompt, x_sample):
        L = x.shape[1]
        rope, zt = _position_tables(L)
        outs.append(_trunk(x, layers, mixers, _fft_plan(L), rope, zt))
    return tuple(outs)
```

```python
import functools
import math

import jax
import jax.numpy as jnp
from jax import lax
from jax.experimental import pallas as pl
from jax.experimental.pallas import tpu as pltpu

BF = jnp.bfloat16
F32 = jnp.float32

D_MODEL = 1024
N_HEADS = 8
QK_NOPE = 64
QK_ROPE = 32
QK_DIM = QK_NOPE + QK_ROPE
V_DIM = 64
Q_LORA = 256
KV_LORA = 128
ATTN_W = N_HEADS * V_DIM
HY_W = D_MODEL - ATTN_W
ROPE_THETA = 10000.0
N_BANDS = 16
POS_EMB = 1 + 2 * N_BANDS
FILT_HID = 64
N_EXPERTS = 8
EPS = 1e-6

LANES = 128
HEAD_PAD = LANES
PAD_W = N_HEADS * HEAD_PAD
VMEM_LIMIT = 56 * 1024 * 1024
HI = lax.Precision.HIGHEST
ATTN_TQ = 256
ATTN_KV_BYTES = 8 * 1024 * 1024
MOE_TM = 1024
MOE_SUBTILES = 2
MOE_VMEM_LIMIT = 60 * 1024 * 1024
HY_CB = 16
HY_DT = F32


def _cparams(sem, vmem=VMEM_LIMIT):
    return pltpu.CompilerParams(dimension_semantics=sem, vmem_limit_bytes=vmem)


def _rms(x, g):
    ms = jnp.mean(x * x, axis=-1, keepdims=True)
    return x * lax.rsqrt(ms + EPS) * g


def _dot(a, b):
    return jnp.dot(a, b, preferred_element_type=F32)


def _full(a):
    return pl.BlockSpec(a.shape, lambda *_: (0,) * a.ndim)


def _in_proj_kernel(x_ref, gmix_ref, wa_ref, cqn_ref, ckvn_ref, wuq_ref, wuqs_ref, wuk_ref, wuv_ref, pkr_ref,
                    pkrs_ref, cq_ref, sq_ref, ck_ref, sk_ref, q_ref, k_ref, v_ref):
    h = _rms(x_ref[...], gmix_ref[...]).astype(BF)
    pa = _dot(h, wa_ref[...])
    c_q = pa[:, 0:Q_LORA]
    c_kv = pa[:, Q_LORA:Q_LORA + KV_LORA]
    k_r = pa[:, Q_LORA + KV_LORA:Q_LORA + KV_LORA + QK_ROPE].astype(BF)
    cqn = _rms(c_q, cqn_ref[...]).astype(BF)
    ckvn = _rms(c_kv, ckvn_ref[...]).astype(BF)
    q = _dot(cqn, wuq_ref[...])
    k = _dot(ckvn, wuk_ref[...]) + _dot(k_r, pkr_ref[...])
    q_sw = _dot(cqn, wuqs_ref[...])
    k_sw = _dot(k_r, pkrs_ref[...])
    lane = lax.broadcasted_iota(jnp.int32, (1, PAD_W), 1)
    ones_col = jnp.where(lane % HEAD_PAD == V_DIM, 1.0, 0.0)
    v_ref[...] = (_dot(ckvn, wuv_ref[...]) + ones_col).astype(BF)
    for src, swp, cref, sref, oref in ((q, q_sw, cq_ref, sq_ref, q_ref), (k, k_sw, ck_ref, sk_ref, k_ref)):
        ct = cref[...]
        st = sref[...]
        for hh in range(N_HEADS):
            sl = slice(HEAD_PAD * hh, HEAD_PAD * (hh + 1))
            t = src[:, sl]
            ms = jnp.sum(t * t, axis=-1, keepdims=True) * (1.0 / QK_DIM)
            oref[:, sl] = ((t * ct + swp[:, sl] * st) * lax.rsqrt(ms + EPS)).astype(BF)


def _in_proj(x2d, L, lw, rope, tm):
    T = x2d.shape[0]
    tps = L // tm
    weights = [lw['mix_norm'], lw['w_a'], lw['cq_norm'], lw['ckv_norm'],
               lw['w_uq'], lw['w_uq_sw'], lw['w_uk'], lw['w_uv'], lw['p_kr'], lw['p_kr_sw']]
    cos_t, sin_t = rope
    tables = [cos_t * lw['gq_cos'], sin_t * lw['gq_sin'], cos_t * lw['gk_cos'], sin_t * lw['gk_sin']]
    in_specs = ([pl.BlockSpec((tm, D_MODEL), lambda i: (i, 0))] + [_full(w) for w in weights]
                + [pl.BlockSpec((tm, LANES), lambda i: (i % tps, 0))] * 4)
    return pl.pallas_call(
        _in_proj_kernel,
        out_shape=[jax.ShapeDtypeStruct((T, PAD_W), BF)] * 3, grid=(T // tm,), in_specs=in_specs,
        out_specs=[pl.BlockSpec((tm, PAD_W), lambda i: (i, 0))] * 3,
        compiler_params=_cparams(("parallel",)), name="in_proj",
    )(x2d, *weights, *tables)


def _hy_proj_kernel(x_ref, gmix_ref, wt_ref, v_ref, x1_ref, x2_ref):
    h = _rms(x_ref[...], gmix_ref[...]).astype(BF)
    for g, oref in enumerate((v_ref, x1_ref, x2_ref)):
        ut = lax.dot_general(wt_ref[g * HY_W:(g + 1) * HY_W, :], h, (((1,), (1,)), ((), ())),
                             preferred_element_type=F32)
        for a in range(oref.shape[1]):
            oref[:, a, :] = ut[:, a * LANES:(a + 1) * LANES].astype(oref.dtype)


def _hy_proj(x2d, B, L, lw, tm):
    tps = L // tm
    rows = tm // LANES
    shape = jax.ShapeDtypeStruct((B, HY_W, L // LANES, LANES), HY_DT)
    spec = pl.BlockSpec((None, HY_W, rows, LANES), lambda i: (i // tps, 0, i % tps, 0))
    return pl.pallas_call(
        _hy_proj_kernel, out_shape=[shape] * 3, grid=(B * tps,),
        in_specs=[pl.BlockSpec((tm, D_MODEL), lambda i: (i, 0)), _full(lw['mix_norm']), _full(lw['w_hy_t'])],
        out_specs=[spec] * 3,
        compiler_params=_cparams(("parallel",)), name="hy_proj",
    )(x2d, lw['mix_norm'], lw['w_hy_t'])


def _attn_kernel(q_ref, k_ref, v_ref, o_ref, *, heads):
    lane = lax.broadcasted_iota(jnp.int32, (1, HEAD_PAD), 1)
    for h in range(heads):
        sl = slice(HEAD_PAD * h, HEAD_PAD * (h + 1))
        s = lax.dot_general(q_ref[:, sl], k_ref[:, sl], (((1,), (1,)), ((), ())), preferred_element_type=F32)
        m = jnp.max(s, axis=-1, keepdims=True)
        p = jnp.exp2(s - m).astype(BF)
        a = _dot(p, v_ref[:, sl])
        o = jnp.where(lane < V_DIM, a / a[:, V_DIM:V_DIM + 1], 0.0)
        o_ref[:, sl] = o.astype(o_ref.dtype)


def _attn(q, k, v, B, L, tq):
    heads = max(h for h in (1, 2, 4) if h == 1 or L * h * HEAD_PAD * 2 <= ATTN_KV_BYTES)
    kv_mode = dict(pipeline_mode=pl.Buffered(1)) if L * heads * HEAD_PAD * 2 > ATTN_KV_BYTES // 2 else {}
    nq = L // tq
    w = heads * HEAD_PAD
    return pl.pallas_call(
        functools.partial(_attn_kernel, heads=heads),
        out_shape=jax.ShapeDtypeStruct(q.shape, BF),
        grid=(B, N_HEADS // heads, nq),
        in_specs=[
            pl.BlockSpec((tq, w), lambda b, h, qi: (b * nq + qi, h)),
            pl.BlockSpec((L, w), lambda b, h, qi: (b, h), **kv_mode),
            pl.BlockSpec((L, w), lambda b, h, qi: (b, h), **kv_mode),
        ],
        out_specs=pl.BlockSpec((tq, w), lambda b, h, qi: (b * nq + qi, h)),
        compiler_params=_cparams(("parallel", "parallel", "parallel")), name="attn",
    )(q, k, v)


def _filter_kernel(z_ref, w1_ref, b1_ref, fr_ref, w2_ref, b2_ref, w3_ref, dec_ref, o_ref, *, seq_len):
    z = z_ref[...]
    fr = fr_ref[...]
    h = jnp.sin(fr * (jnp.dot(w1_ref[...], z, precision=HI, preferred_element_type=F32) + b1_ref[...]))
    h = jnp.sin(fr * (jnp.dot(w2_ref[...], h, precision=HI, preferred_element_type=F32) + b2_ref[...]))
    t = z[0:1, :]
    o = jnp.dot(w3_ref[...], h, precision=HI, preferred_element_type=F32) * jnp.exp(-t * jnp.abs(dec_ref[...]))
    n = pl.program_id(0) * z.shape[1] + lax.broadcasted_iota(jnp.int32, (1, z.shape[1]), 1)
    o = jnp.where(n == seq_len, 0.0, o)
    for a in range(o_ref.shape[1]):
        o_ref[:, a, :] = o[:, a * LANES:(a + 1) * LANES]


def _filter_taps(zt, lw, L, tl):
    n = 2 * L
    ch = lw['filt_w3_t'].shape[1]
    half_tiles = L // tl
    sel = lambda i: (i // half_tiles, 0, 0)
    ws = [lw['filt_w1_t'], lw['filt_b1'], lw['filt_freq'], lw['filt_w2_t'], lw['filt_b2']]
    return pl.pallas_call(
        functools.partial(_filter_kernel, seq_len=L),
        out_shape=jax.ShapeDtypeStruct((ch, n // LANES, LANES), F32),
        grid=(n // tl,),
        in_specs=[pl.BlockSpec((LANES, tl), lambda i: (0, i))] + [_full(w) for w in ws]
        + [pl.BlockSpec((None, ch, FILT_HID), sel), pl.BlockSpec((None, ch, 1), sel)],
        out_specs=pl.BlockSpec((ch, tl // LANES, LANES), lambda i: (0, i, 0)),
        compiler_params=_cparams(("parallel",)), name="filt_taps",
    )(zt, *ws, lw['filt_w3_t'], lw['filt_decay_t'])


def _lanes_of(parts):
    return parts[0] if len(parts) == 1 else jnp.concatenate(parts, axis=1)


def _hyena_kernel(v_ref, x1_ref, x2_ref, kt_ref, cw_ref, cb_ref, d_ref, msig_ref, mfilt_ref, minv_ref,
                  g2f_ref, g2i_ref, twr_ref, twi_ref, o_ref, kf_sc, *, scale):
    _, cb, nh, _ = v_ref.shape
    n1 = 2 * nh
    twr = twr_ref[...]
    twi = twi_ref[...]

    def fwd(s_all, m):
        a = _dot(m, s_all)
        rows = []
        for c in range(cb):
            ar = a[:n1, c * LANES:(c + 1) * LANES]
            ai = a[n1:, c * LANES:(c + 1) * LANES]
            rows.append(jnp.concatenate([ar * twr - ai * twi, ar * twi + ai * twr], axis=1).astype(BF))
        return _dot(jnp.concatenate(rows, axis=0), g2f_ref[...])

    def inv(z):
        b = _dot(z.astype(BF), g2i_ref[...])
        cols = []
        for c in range(cb):
            br = b[c * n1:(c + 1) * n1, :LANES]
            bi = b[c * n1:(c + 1) * n1, LANES:]
            cols.append(jnp.concatenate([br * twr + bi * twi, bi * twr - br * twi], axis=0).astype(BF))
        return _dot(minv_ref[...], _lanes_of(cols))

    def cmul(x, kf):
        xr, xi = x[:, :LANES], x[:, LANES:]
        kr, ki = kf[:, :LANES], kf[:, LANES:]
        return jnp.concatenate([xr * kr - xi * ki, xr * ki + xi * kr], axis=1)

    @pl.when(pl.program_id(1) == 0)
    def _():
        for o in range(2):
            taps = _lanes_of([kt_ref[o, c] for c in range(cb)]).astype(BF)
            kf_sc[o] = fwd(taps, mfilt_ref[...]) * scale

    row = lax.broadcasted_iota(jnp.int32, (cb * nh, LANES), 0) % nh
    lane = lax.broadcasted_iota(jnp.int32, (cb * nh, LANES), 1)

    def short_conv(ref, g):
        w = [jnp.broadcast_to(cw_ref[k, g], (cb, nh, LANES)).reshape(cb * nh, LANES) for k in range(3)]
        bias = jnp.broadcast_to(cb_ref[g], (cb, nh, LANES)).reshape(cb * nh, LANES)
        halves = []
        for r in range(2):
            u = ref[r].astype(F32).reshape(cb * nh, LANES)
            lp = pltpu.roll(u, 1, 1)
            prev = jnp.where(lane == 0, pltpu.roll(lp, 1, 0), lp)
            prev = jnp.where((lane == 0) & (row == 0), 0.0, prev)
            ln = pltpu.roll(u, LANES - 1, 1)
            nxt = jnp.where(lane == LANES - 1, pltpu.roll(ln, cb * nh - 1, 0), ln)
            nxt = jnp.where((lane == LANES - 1) & (row == nh - 1), 0.0, nxt)
            y = prev * w[0] + u * w[1] + nxt * w[2] + bias
            halves.append(_lanes_of([y[c * nh:(c + 1) * nh] for c in range(cb)]))
        return jnp.concatenate(halves, axis=0)

    def skip(o):
        return _lanes_of([d_ref[o, c] for c in range(cb)])

    v = short_conv(v_ref, 0)
    x1 = short_conv(x1_ref, 1)
    x2 = short_conv(x2_ref, 2)
    y = inv(cmul(fwd(v.astype(BF), msig_ref[...]), kf_sc[0]))
    z1 = (x1 * (y + skip(0) * v)).astype(BF)
    y = inv(cmul(fwd(z1, msig_ref[...]), kf_sc[1]))
    hy = x2 * (y + skip(1) * z1.astype(F32))
    for c in range(cb):
        o_ref[0, :, c, :] = hy[:nh, c * LANES:(c + 1) * LANES].astype(o_ref.dtype)
        o_ref[1, :, c, :] = hy[nh:, c * LANES:(c + 1) * LANES].astype(o_ref.dtype)


def _hyena(hv, hx1, hx2, taps, lw, plan, B, L):
    nh = L // LANES
    n1 = 2 * nh
    P = B // 2
    cb = HY_CB
    pair = lambda a: a.reshape(P, 2, HY_W, nh, LANES)
    sig = pl.BlockSpec((None, 2, cb, nh, LANES), lambda j, p: (p, 0, j, 0, 0))
    consts = [plan['m_sig'], plan['m_filt'], plan['m_inv'], plan['g2f'], plan['g2i'], plan['twr'], plan['twi']]
    out = pl.pallas_call(
        functools.partial(_hyena_kernel, scale=1.0 / (2 * L)),
        out_shape=jax.ShapeDtypeStruct((P, 2, nh, HY_W, LANES), HY_DT),
        grid=(HY_W // cb, P),
        in_specs=[sig, sig, sig,
                  pl.BlockSpec((2, cb, n1, LANES), lambda j, p: (0, j, 0, 0)),
                  pl.BlockSpec((3, 3, cb, 1, LANES), lambda j, p: (0, 0, j, 0, 0)),
                  pl.BlockSpec((3, cb, 1, LANES), lambda j, p: (0, j, 0, 0)),
                  pl.BlockSpec((2, cb, 1, LANES), lambda j, p: (0, j, 0, 0))] + [_full(c) for c in consts],
        out_specs=pl.BlockSpec((None, 2, nh, cb, LANES), lambda j, p: (p, 0, 0, j, 0)),
        scratch_shapes=[pltpu.VMEM((2, cb * n1, 2 * LANES), F32)],
        compiler_params=_cparams(("parallel", "arbitrary")), name="hyena",
    )(pair(hv), pair(hx1), pair(hx2), taps.reshape(2, HY_W, n1, LANES), lw['conv_w_t'], lw['conv_b_t'],
      lw['hyena_d_t'], *consts)
    return out.reshape(B, nh, HY_W, LANES)


def _fft_plan(L):
    n = 2 * L
    n2 = LANES
    n1 = n // n2
    i1 = jnp.arange(n1, dtype=jnp.int32)
    i2 = jnp.arange(n2, dtype=jnp.int32)
    a1 = (2.0 * math.pi / n1) * ((i1[:, None] * i1[None, :]) % n1).astype(F32)
    c1, s1 = jnp.cos(a1), jnp.sin(a1)
    hn = n1 // 2
    m_sig = jnp.concatenate([jnp.concatenate([c1[:, :hn], s1[:, :hn]], 1),
                             jnp.concatenate([-s1[:, :hn], c1[:, :hn]], 1)], 0).astype(BF)
    m_filt = jnp.concatenate([c1, -s1], 0).astype(BF)
    m_inv = jnp.concatenate([jnp.concatenate([c1[:hn], -s1[:hn]], 1),
                             jnp.concatenate([s1[:hn], c1[:hn]], 1)], 0).astype(BF)
    a2 = (2.0 * math.pi / n2) * ((i2[:, None] * i2[None, :]) % n2).astype(F32)
    c2, s2 = jnp.cos(a2), jnp.sin(a2)
    g2f = jnp.concatenate([jnp.concatenate([c2, -s2], 1), jnp.concatenate([s2, c2], 1)], 0).astype(BF)
    g2i = jnp.concatenate([jnp.concatenate([c2, s2], 1), jnp.concatenate([-s2, c2], 1)], 0).astype(BF)
    at = (2.0 * math.pi / n) * (i1[:, None] * i2[None, :]).astype(F32)
    return dict(m_sig=m_sig, m_filt=m_filt, m_inv=m_inv, g2f=g2f, g2i=g2i, twr=jnp.cos(at), twi=-jnp.sin(at))


def _pick(total, want):
    b = min(total, want)
    while total % b:
        b //= 2
    return b


def _out_proj_kernel(x_ref, a_ref, hy_ref, ga_ref, gh_ref, wa_ref, wh_ref, o_ref):
    a = a_ref[...].astype(F32)
    ra = lax.rsqrt(jnp.sum(a * a, axis=-1, keepdims=True) * (1.0 / ATTN_W) + EPS)
    an = (a * ra * ga_ref[...]).astype(BF)
    acc = x_ref[...] + _dot(an, wa_ref[...])
    gh = gh_ref[...]
    wh = wh_ref[...]
    for s in range(hy_ref.shape[0]):
        hy = hy_ref[s].astype(F32)
        rh = lax.rsqrt(jnp.sum(hy * hy, axis=0, keepdims=True) * (1.0 / HY_W) + EPS)
        hn = (hy * rh * gh).astype(BF)
        part = lax.dot_general(hn, wh, (((0,), (0,)), ((), ())), preferred_element_type=F32)
        o_ref[s * LANES:(s + 1) * LANES, :] = acc[s * LANES:(s + 1) * LANES, :] + part


def _out_proj(x2d, attn, hy, lw, B, L, tm):
    T = x2d.shape[0]
    tps = L // tm
    ws = [lw['ga'], lw['gh_t'], lw['w_out_a'], lw['w_out_h']]
    return pl.pallas_call(
        _out_proj_kernel,
        out_shape=jax.ShapeDtypeStruct((T, D_MODEL), F32),
        grid=(T // tm,),
        in_specs=[pl.BlockSpec((tm, D_MODEL), lambda i: (i, 0)),
                  pl.BlockSpec((tm, PAD_W), lambda i: (i, 0)),
                  pl.BlockSpec((None, tm // LANES, HY_W, LANES), lambda i: (i // tps, i % tps, 0, 0))]
        + [_full(w) for w in ws],
        out_specs=pl.BlockSpec((tm, D_MODEL), lambda i: (i, 0)),
        compiler_params=_cparams(("parallel",)), name="out_proj",
    )(x2d, attn, hy, *ws)


def _silu_mul(g, u):
    return g * (1.0 / (1.0 + jnp.exp(-g))) * u


def _ffn_kernel(x_ref, g_ref, wg_ref, wu_ref, wd_ref, o_ref, hn_sc, acc_sc):
    j = pl.program_id(1)

    @pl.when(j == 0)
    def _():
        hn_sc[...] = _rms(x_ref[...], g_ref[...]).astype(BF)
        acc_sc[...] = jnp.zeros(acc_sc.shape, F32)

    hn = hn_sc[...]
    a = _silu_mul(_dot(hn, wg_ref[...]), _dot(hn, wu_ref[...])).astype(BF)
    acc_sc[...] += _dot(a, wd_ref[...])

    @pl.when(j == pl.num_programs(1) - 1)
    def _():
        o_ref[...] = x_ref[...] + acc_sc[...]


def _ffn(x2d, g, wg, wu, wd, tm, tf):
    T = x2d.shape[0]
    ff = wg.shape[1]
    return pl.pallas_call(
        _ffn_kernel,
        out_shape=jax.ShapeDtypeStruct((T, D_MODEL), F32),
        grid=(T // tm, ff // tf),
        in_specs=[pl.BlockSpec((tm, D_MODEL), lambda i, j: (i, 0)),
                  pl.BlockSpec((1, D_MODEL), lambda i, j: (0, 0)),
                  pl.BlockSpec((D_MODEL, tf), lambda i, j: (0, j)),
                  pl.BlockSpec((D_MODEL, tf), lambda i, j: (0, j)),
                  pl.BlockSpec((tf, D_MODEL), lambda i, j: (j, 0))],
        out_specs=pl.BlockSpec((tm, D_MODEL), lambda i, j: (i, 0)),
        scratch_shapes=[pltpu.VMEM((tm, D_MODEL), BF), pltpu.VMEM((tm, D_MODEL), F32)],
        compiler_params=_cparams(("parallel", "arbitrary")), name="ffn",
    )(x2d, g, wg, wu, wd)


def _moe_kernel(x_ref, g_ref, rwt_ref, wg_ref, wu_ref, wd_ref, o_ref, hn_sc, sel_sc, pos_sc, gate_sc,
                oh_sc, g_sc, xg_sc, y_sc, *, cap):
    e = pl.program_id(1)
    j = pl.program_id(2)
    nsub, tm = sel_sc.shape[0], sel_sc.shape[2]
    last_j = pl.num_programs(2) - 1
    slot = lax.broadcasted_iota(jnp.int32, (cap, tm), 0).astype(F32)

    def expert(xg):
        h = _silu_mul(_dot(xg, wg_ref[...]), _dot(xg, wu_ref[...])).astype(BF)
        return _dot(h, wd_ref[...])

    subtiles = []
    for s in range(nsub):
        rows = slice(s * tm, (s + 1) * tm)

        @pl.when((e == 0) & (j == 0))
        def _():
            x = x_ref[rows, :]
            hn = _rms(x, g_ref[...])
            hn_sc[s] = hn.astype(BF)
            o_ref[rows, :] = x
            lg = lax.dot_general(rwt_ref[...], hn, (((1,), (1,)), ((), ())), precision=HI,
                                 preferred_element_type=F32)
            row = lax.broadcasted_iota(jnp.int32, lg.shape, 0)
            m1 = jnp.max(lg, axis=0, keepdims=True)
            i1 = jnp.min(jnp.where(lg == m1, row, N_EXPERTS), axis=0, keepdims=True)
            lg2 = jnp.where(row == i1, -jnp.inf, lg)
            m2 = jnp.max(lg2, axis=0, keepdims=True)
            i2 = jnp.min(jnp.where(lg2 == m2, row, N_EXPERTS), axis=0, keepdims=True)
            e2 = jnp.exp(m2 - m1)
            g1 = 1.0 / (1.0 + e2)
            gate_sc[s] = jnp.where(row == i1, g1, 0.0) + jnp.where(row == i2, e2 * g1, 0.0)
            sel = jnp.where((row == i1) | (row == i2), 1.0, 0.0)
            sel_sc[s] = sel
            r = lax.broadcasted_iota(jnp.int32, (tm, tm), 0)
            c = lax.broadcasted_iota(jnp.int32, (tm, tm), 1)
            pos_sc[s] = _dot(sel.astype(BF), jnp.where(r < c, 1.0, 0.0).astype(BF))

        sel_row = sel_sc[s, pl.ds(e, 1), :]
        pos_row = pos_sc[s, pl.ds(e, 1), :]
        gate_row = gate_sc[s, pl.ds(e, 1), :]
        count = jnp.sum(sel_row).astype(jnp.int32)

        def one_hot(c, sel_row=sel_row, pos_row=pos_row, gate_row=gate_row):
            base = jnp.asarray(c * cap).astype(F32)
            onehot = jnp.where((pos_row - base == slot) & (sel_row > 0.0), 1.0, 0.0)
            gate = jnp.sum(onehot * gate_row, axis=1, keepdims=True)
            return onehot.astype(BF), gate

        def scatter(onehot, y, rows=rows):
            o_ref[rows, :] += lax.dot_general(onehot, y.astype(BF), (((0,), (0,)), ((), ())),
                                              preferred_element_type=F32)

        @pl.when(j == 0)
        def _():
            onehot, gate = one_hot(0)
            oh_sc[s] = onehot
            g_sc[s] = gate
            xg_sc[s * cap:(s + 1) * cap, :] = _dot(onehot, hn_sc[s]).astype(BF)

        subtiles.append((s, count, one_hot, scatter))

    @pl.when(j == 0)
    def _():
        y_sc[...] = jnp.zeros(y_sc.shape, F32)

    y_sc[...] += expert(xg_sc[...])

    for s, count, one_hot, scatter in subtiles:
        @pl.when(j == last_j)
        def _():
            scatter(oh_sc[s], y_sc[s * cap:(s + 1) * cap, :] * g_sc[s])

        def chunk(c, carry, s=s, one_hot=one_hot, scatter=scatter):
            onehot, gate = one_hot(c)
            xg = _dot(onehot, hn_sc[s]).astype(BF)
            scatter(onehot, expert(xg) * gate)
            return carry

        lax.fori_loop(1, (count + cap - 1) // cap, chunk, 0)


def _moe(x2d, g, rwt, wg, wu, wd, tm, nsub, tf, cap):
    T = x2d.shape[0]
    ne, _, ff = wg.shape
    tb = tm * nsub
    return pl.pallas_call(
        functools.partial(_moe_kernel, cap=cap),
        out_shape=jax.ShapeDtypeStruct((T, D_MODEL), F32),
        grid=(T // tb, ne, ff // tf),
        in_specs=[pl.BlockSpec((tb, D_MODEL), lambda i, e, j: (i, 0), pipeline_mode=pl.Buffered(1)),
                  pl.BlockSpec((1, D_MODEL), lambda i, e, j: (0, 0)),
                  pl.BlockSpec((ne, D_MODEL), lambda i, e, j: (0, 0)),
                  pl.BlockSpec((None, D_MODEL, tf), lambda i, e, j: (e, 0, j)),
                  pl.BlockSpec((None, D_MODEL, tf), lambda i, e, j: (e, 0, j)),
                  pl.BlockSpec((None, tf, D_MODEL), lambda i, e, j: (e, j, 0))],
        out_specs=pl.BlockSpec((tb, D_MODEL), lambda i, e, j: (i, 0)),
        scratch_shapes=[pltpu.VMEM((nsub, tm, D_MODEL), BF), pltpu.VMEM((nsub, ne, tm), F32),
                        pltpu.VMEM((nsub, ne, tm), F32), pltpu.VMEM((nsub, ne, tm), F32),
                        pltpu.VMEM((nsub, cap, tm), BF), pltpu.VMEM((nsub, cap, 1), F32),
                        pltpu.VMEM((nsub * cap, D_MODEL), BF), pltpu.VMEM((nsub * cap, D_MODEL), F32)],
        compiler_params=_cparams(("parallel", "arbitrary", "arbitrary"), MOE_VMEM_LIMIT), name="moe",
    )(x2d, g, rwt, wg, wu, wd)


def _pad_heads(w, d):
    lead = w.shape[:-1]
    w = w.reshape(lead + (N_HEADS, d))
    w = jnp.pad(w, [(0, 0)] * len(lead) + [(0, 0), (0, HEAD_PAD - d)])
    return w.reshape(lead + (PAD_W,))


def _per_channel(a):
    return jnp.broadcast_to(a.astype(F32)[..., None, None], a.shape + (1, LANES))


def _layer_weights(p, layer):
    (mix_norm, w_in, cq_norm, ckv_norm, w_uq, w_ukv, q_norm, k_norm, conv_w, conv_b,
     filt_w1, filt_b1, filt_freq, filt_w2, filt_b2, filt_w3, filt_decay, hyena_d,
     out_norm, w_out, ffn_norm) = [a[layer] for a in p]
    na = Q_LORA + KV_LORA + QK_ROPE
    ukv = w_ukv.reshape(KV_LORA, N_HEADS, QK_NOPE + V_DIM)
    eye = jnp.eye(QK_ROPE, dtype=F32)
    p_kr = jnp.pad(eye, ((0, 0), (QK_NOPE, HEAD_PAD - QK_DIM)))
    row = lambda a: a.reshape(1, -1).astype(F32)
    col = lambda a: a.reshape(-1, 1).astype(F32)
    half = QK_ROPE // 2
    uq = w_uq.reshape(Q_LORA, N_HEADS, QK_DIM)
    swap = lambda a: jnp.concatenate([a[..., half:], a[..., :half]], -1)
    swap_gain = lambda g: jnp.concatenate([g[:QK_NOPE], swap(g[QK_NOPE:])])
    gain_row = lambda g: jnp.pad(g, (0, HEAD_PAD - QK_DIM)).reshape(1, HEAD_PAD).astype(F32)
    q_scale = QK_DIM ** -0.5 * math.log2(math.e)
    w_out_a = w_out[:ATTN_W].reshape(N_HEADS, V_DIM, D_MODEL)
    w_out_a = jnp.pad(w_out_a, ((0, 0), (0, HEAD_PAD - V_DIM), (0, 0))).reshape(PAD_W, D_MODEL)
    w3 = filt_w3.reshape(FILT_HID, 2, 2, HY_W).transpose(2, 1, 3, 0).reshape(2, 2 * HY_W, FILT_HID)
    dec = filt_decay.reshape(2, 2, HY_W).transpose(1, 0, 2).reshape(2, 2 * HY_W, 1)
    return dict(
        mix_norm=row(mix_norm), w_a=w_in[:, :na].astype(BF), w_hy_t=w_in[:, na:].T.astype(BF),
        cq_norm=row(cq_norm), ckv_norm=row(ckv_norm),
        w_uq=_pad_heads(w_uq, QK_DIM).astype(BF),
        w_uk=_pad_heads(ukv[:, :, :QK_NOPE].reshape(KV_LORA, -1), QK_NOPE).astype(BF),
        w_uv=_pad_heads(ukv[:, :, QK_NOPE:].reshape(KV_LORA, -1), V_DIM).astype(BF),
        w_uq_sw=_pad_heads(jnp.concatenate([jnp.zeros_like(uq[:, :, :QK_NOPE]), swap(uq[:, :, QK_NOPE:])], -1)
                           .reshape(Q_LORA, -1), QK_DIM).astype(BF),
        p_kr=jnp.tile(p_kr, (1, N_HEADS)).astype(BF),
        p_kr_sw=jnp.tile(jnp.pad(swap(eye), ((0, 0), (QK_NOPE, HEAD_PAD - QK_DIM))), (1, N_HEADS)).astype(BF),
        gq_cos=gain_row(q_norm) * q_scale, gq_sin=gain_row(swap_gain(q_norm)) * q_scale,
        gk_cos=gain_row(k_norm), gk_sin=gain_row(swap_gain(k_norm)),
        conv_w_t=_per_channel(conv_w.reshape(3, 3, HY_W)), conv_b_t=_per_channel(conv_b.reshape(3, HY_W)),
        filt_w1_t=jnp.pad(filt_w1, ((0, LANES - POS_EMB), (0, 0))).T, filt_b1=col(filt_b1),
        filt_freq=col(filt_freq), filt_w2_t=filt_w2.T, filt_b2=col(filt_b2), filt_w3_t=w3, filt_decay_t=dec,
        hyena_d_t=_per_channel(hyena_d),
        ga=_pad_heads(out_norm[:ATTN_W], V_DIM).reshape(1, PAD_W),
        gh_t=jnp.broadcast_to(col(out_norm[ATTN_W:]), (HY_W, LANES)),
        w_out_a=w_out_a.astype(BF), w_out_h=w_out[ATTN_W:].astype(BF),
        ffn_norm=row(ffn_norm),
    )


def _position_tables(L):
    pos = jnp.arange(L, dtype=F32)
    half = QK_ROPE // 2
    inv = ROPE_THETA ** (-jnp.arange(0, QK_ROPE, 2, dtype=F32) / QK_ROPE)
    ang = pos[:, None] * inv[None, :]
    cos, sin = jnp.cos(ang), jnp.sin(ang)
    zeros = lambda w: jnp.zeros((L, w), F32)
    cos_t = jnp.concatenate([jnp.ones((L, QK_NOPE), F32), cos, cos, zeros(HEAD_PAD - QK_DIM)], 1)
    sin_t = jnp.concatenate([zeros(QK_NOPE), -sin, sin, zeros(HEAD_PAD - QK_DIM)], 1)
    n = jnp.arange(2 * L, dtype=jnp.int32)
    p = jnp.where(n < L, n, 2 * L - n)
    p = jnp.where(n == L, 0, p)
    t = jnp.take(jnp.linspace(0.0, 1.0, L, dtype=F32), p)[None, :]
    w = 2.0 * math.pi * p.astype(F32) / L
    f = jnp.linspace(1e-4, N_BANDS - 1, N_BANDS, dtype=F32)
    a = f[:, None] * w[None, :]
    zt = jnp.concatenate([t, jnp.cos(a), -jnp.sin(a), jnp.zeros((LANES - POS_EMB, 2 * L), F32)], axis=0)
    return (cos_t, sin_t), zt


def _moe_capacity(tm):
    return max(16, (tm * 9 // 32) // 16 * 16)


def _trunk(x, layers, mixers, plan, rope, zt):
    B, L, _ = x.shape
    T = B * L
    x2d = x.reshape(T, D_MODEL)
    tm = _pick(L, 512)
    for lw, mixer in zip(layers, mixers):
        q, k, v = _in_proj(x2d, L, lw, rope, tm)
        hv, hx1, hx2 = _hy_proj(x2d, B, L, lw, _pick(L, 2048))
        attn = _attn(q, k, v, B, L, _pick(L, ATTN_TQ))
        taps = _filter_taps(zt, lw, L, _pick(L, 1024))
        hy = _hyena(hv, hx1, hx2, taps, lw, plan, B, L)
        x2d = _out_proj(x2d, attn, hy, lw, B, L, tm)
        tf = 1408
        if mixer[0] == 'dense':
            _, wg, wu, wd = mixer
            x2d = _ffn(x2d, lw['ffn_norm'], wg, wu, wd, _pick(T, 1024), tf)
        else:
            _, rwt, wg, wu, wd = mixer
            tme = _pick(T, MOE_TM)
            x2d = _moe(x2d, lw['ffn_norm'], rwt, wg, wu, wd, tme, _pick(T // tme, MOE_SUBTILES), tf,
                       _moe_capacity(tme))
    return x2d.reshape(B, L, D_MODEL)


def kernel(x_prompt, x_sample, mix_norm, w_in, cq_norm, ckv_norm, w_uq, w_ukv, q_norm, k_norm, conv_w, conv_b, filt_w1, filt_b1, filt_freq, filt_w2, filt_b2, filt_w3, filt_decay, hyena_d, out_norm, w_out, ffn_norm, dense_wg, dense_wu, dense_wd, router_w, moe_wg, moe_wu, moe_wd):
    per_layer = (mix_norm, w_in, cq_norm, ckv_norm, w_uq, w_ukv, q_norm, k_norm, conv_w, conv_b,
                 filt_w1, filt_b1, filt_freq, filt_w2, filt_b2, filt_w3, filt_decay, hyena_d,
                 out_norm, w_out, ffn_norm)
    depth = mix_norm.shape[0]
    layers = [_layer_weights(per_layer, l) for l in range(depth)]
    mixers = []
    for l in range(depth):
        j = l // 2
        if l % 2 == 0:
            mixers.append(('dense', dense_wg[j].astype(BF), dense_wu[j].astype(BF), dense_wd[j].astype(BF)))
        else:
            mixers.append(('moe', router_w[j].T, moe_wg[j].astype(BF), moe_wu[j].astype(BF), moe_wd[j].astype(BF)))
    outs = []
    for x in (x_prompt, x_sample):
        L = x.shape[1]
        rope, zt = _position_tables(L)
        outs.append(_trunk(x, layers, mixers, _fft_plan(L), rope, zt))
    return tuple(outs)
```

```python
import functools
import math

import jax
import jax.numpy as jnp
from jax import lax
from jax.experimental import pallas as pl
from jax.experimental.pallas import tpu as pltpu

BF = jnp.bfloat16
F32 = jnp.float32

D_MODEL = 1024
N_HEADS = 8
QK_NOPE = 64
QK_ROPE = 32
QK_DIM = QK_NOPE + QK_ROPE
V_DIM = 64
Q_LORA = 256
KV_LORA = 128
ATTN_W = N_HEADS * V_DIM
HY_W = D_MODEL - ATTN_W
ROPE_THETA = 10000.0
N_BANDS = 16
POS_EMB = 1 + 2 * N_BANDS
FILT_HID = 64
N_EXPERTS = 8
EPS = 1e-6

LANES = 128
HEAD_PAD = LANES
PAD_W = N_HEADS * HEAD_PAD
VMEM_LIMIT = 56 * 1024 * 1024
HI = lax.Precision.HIGHEST
ATTN_TQ = 256
ATTN_KV_BYTES = 8 * 1024 * 1024
MOE_TM = 1024
MOE_SUBTILES = 2
MOE_VMEM_LIMIT = 60 * 1024 * 1024
HY_CB = 16
HY_DT = F32


def _cparams(sem, vmem=VMEM_LIMIT):
    return pltpu.CompilerParams(dimension_semantics=sem, vmem_limit_bytes=vmem)


def _rms(x, g):
    ms = jnp.mean(x * x, axis=-1, keepdims=True)
    return x * lax.rsqrt(ms + EPS) * g


def _dot(a, b):
    return jnp.dot(a, b, preferred_element_type=F32)


def _full(a):
    return pl.BlockSpec(a.shape, lambda *_: (0,) * a.ndim)


def _in_proj_kernel(x_ref, gmix_ref, wa_ref, cqn_ref, ckvn_ref, wuq_ref, wuqs_ref, wuk_ref, wuv_ref, pkr_ref,
                    pkrs_ref, cq_ref, sq_ref, ck_ref, sk_ref, q_ref, k_ref, v_ref):
    h = _rms(x_ref[...], gmix_ref[...]).astype(BF)
    pa = _dot(h, wa_ref[...])
    c_q = pa[:, 0:Q_LORA]
    c_kv = pa[:, Q_LORA:Q_LORA + KV_LORA]
    k_r = pa[:, Q_LORA + KV_LORA:Q_LORA + KV_LORA + QK_ROPE].astype(BF)
    cqn = _rms(c_q, cqn_ref[...]).astype(BF)
    ckvn = _rms(c_kv, ckvn_ref[...]).astype(BF)
    q = _dot(cqn, wuq_ref[...])
    k = _dot(ckvn, wuk_ref[...]) + _dot(k_r, pkr_ref[...])
    q_sw = _dot(cqn, wuqs_ref[...])
    k_sw = _dot(k_r, pkrs_ref[...])
    lane = lax.broadcasted_iota(jnp.int32, (1, PAD_W), 1)
    ones_col = jnp.where(lane % HEAD_PAD == V_DIM, 1.0, 0.0)
    v_ref[...] = (_dot(ckvn, wuv_ref[...]) + ones_col).astype(BF)
    for src, swp, cref, sref, oref in ((q, q_sw, cq_ref, sq_ref, q_ref), (k, k_sw, ck_ref, sk_ref, k_ref)):
        ct = cref[...]
        st = sref[...]
        for hh in range(N_HEADS):
            sl = slice(HEAD_PAD * hh, HEAD_PAD * (hh + 1))
            t = src[:, sl]
            ms = jnp.sum(t * t, axis=-1, keepdims=True) * (1.0 / QK_DIM)
            oref[:, sl] = ((t * ct + swp[:, sl] * st) * lax.rsqrt(ms + EPS)).astype(BF)


def _in_proj(x2d, L, lw, rope, tm):
    T = x2d.shape[0]
    tps = L // tm
    weights = [lw['mix_norm'], lw['w_a'], lw['cq_norm'], lw['ckv_norm'],
               lw['w_uq'], lw['w_uq_sw'], lw['w_uk'], lw['w_uv'], lw['p_kr'], lw['p_kr_sw']]
    cos_t, sin_t = rope
    tables = [cos_t * lw['gq_cos'], sin_t * lw['gq_sin'], cos_t * lw['gk_cos'], sin_t * lw['gk_sin']]
    in_specs = ([pl.BlockSpec((tm, D_MODEL), lambda i: (i, 0))] + [_full(w) for w in weights]
                + [pl.BlockSpec((tm, LANES), lambda i: (i % tps, 0))] * 4)
    return pl.pallas_call(
        _in_proj_kernel,
        out_shape=[jax.ShapeDtypeStruct((T, PAD_W), BF)] * 3, grid=(T // tm,), in_specs=in_specs,
        out_specs=[pl.BlockSpec((tm, PAD_W), lambda i: (i, 0))] * 3,
        compiler_params=_cparams(("parallel",)), name="in_proj",
    )(x2d, *weights, *tables)


def _hy_proj_kernel(x_ref, gmix_ref, wt_ref, v_ref, x1_ref, x2_ref):
    h = _rms(x_ref[...], gmix_ref[...]).astype(BF)
    for g, oref in enumerate((v_ref, x1_ref, x2_ref)):
        ut = lax.dot_general(wt_ref[g * HY_W:(g + 1) * HY_W, :], h, (((1,), (1,)), ((), ())),
                             preferred_element_type=F32)
        for a in range(oref.shape[1]):
            oref[:, a, :] = ut[:, a * LANES:(a + 1) * LANES].astype(oref.dtype)


def _hy_proj(x2d, B, L, lw, tm):
    tps = L // tm
    rows = tm // LANES
    shape = jax.ShapeDtypeStruct((B, HY_W, L // LANES, LANES), HY_DT)
    spec = pl.BlockSpec((None, HY_W, rows, LANES), lambda i: (i // tps, 0, i % tps, 0))
    return pl.pallas_call(
        _hy_proj_kernel, out_shape=[shape] * 3, grid=(B * tps,),
        in_specs=[pl.BlockSpec((tm, D_MODEL), lambda i: (i, 0)), _full(lw['mix_norm']), _full(lw['w_hy_t'])],
        out_specs=[spec] * 3,
        compiler_params=_cparams(("parallel",)), name="hy_proj",
    )(x2d, lw['mix_norm'], lw['w_hy_t'])


def _attn_kernel(q_ref, k_ref, v_ref, o_ref, *, heads):
    lane = lax.broadcasted_iota(jnp.int32, (1, HEAD_PAD), 1)
    for h in range(heads):
        sl = slice(HEAD_PAD * h, HEAD_PAD * (h + 1))
        s = lax.dot_general(q_ref[:, sl], k_ref[:, sl], (((1,), (1,)), ((), ())), preferred_element_type=F32)
        m = jnp.max(s, axis=-1, keepdims=True)
        p = jnp.exp2(s - m).astype(BF)
        a = _dot(p, v_ref[:, sl])
        o = jnp.where(lane < V_DIM, a / a[:, V_DIM:V_DIM + 1], 0.0)
        o_ref[:, sl] = o.astype(o_ref.dtype)


def _attn(q, k, v, B, L, tq):
    heads = max(h for h in (1, 2, 4) if h == 1 or L * h * HEAD_PAD * 2 <= ATTN_KV_BYTES)
    kv_mode = dict(pipeline_mode=pl.Buffered(1)) if L * heads * HEAD_PAD * 2 > ATTN_KV_BYTES // 2 else {}
    nq = L // tq
    w = heads * HEAD_PAD
    return pl.pallas_call(
        functools.partial(_attn_kernel, heads=heads),
        out_shape=jax.ShapeDtypeStruct(q.shape, BF),
        grid=(B, N_HEADS // heads, nq),
        in_specs=[
            pl.BlockSpec((tq, w), lambda b, h, qi: (b * nq + qi, h)),
            pl.BlockSpec((L, w), lambda b, h, qi: (b, h), **kv_mode),
            pl.BlockSpec((L, w), lambda b, h, qi: (b, h), **kv_mode),
        ],
        out_specs=pl.BlockSpec((tq, w), lambda b, h, qi: (b * nq + qi, h)),
        compiler_params=_cparams(("parallel", "parallel", "parallel")), name="attn",
    )(q, k, v)


def _filter_kernel(z_ref, w1_ref, b1_ref, fr_ref, w2_ref, b2_ref, w3_ref, dec_ref, o_ref, *, seq_len):
    z = z_ref[...]
    fr = fr_ref[...]
    h = jnp.sin(fr * (jnp.dot(w1_ref[...], z, precision=HI, preferred_element_type=F32) + b1_ref[...]))
    h = jnp.sin(fr * (jnp.dot(w2_ref[...], h, precision=HI, preferred_element_type=F32) + b2_ref[...]))
    t = z[0:1, :]
    o = _dot(w3_ref[...], h.astype(BF)) * jnp.exp(-t * jnp.abs(dec_ref[...]))
    n = pl.program_id(0) * z.shape[1] + lax.broadcasted_iota(jnp.int32, (1, z.shape[1]), 1)
    o = jnp.where(n == seq_len, 0.0, o)
    for a in range(o_ref.shape[1]):
        o_ref[:, a, :] = o[:, a * LANES:(a + 1) * LANES]


def _filter_taps(zt, lw, L, tl):
    n = 2 * L
    ch = lw['filt_w3_t'].shape[1]
    half_tiles = L // tl
    sel = lambda i: (i // half_tiles, 0, 0)
    ws = [lw['filt_w1_t'], lw['filt_b1'], lw['filt_freq'], lw['filt_w2_t'], lw['filt_b2']]
    return pl.pallas_call(
        functools.partial(_filter_kernel, seq_len=L),
        out_shape=jax.ShapeDtypeStruct((ch, n // LANES, LANES), F32),
        grid=(n // tl,),
        in_specs=[pl.BlockSpec((LANES, tl), lambda i: (0, i))] + [_full(w) for w in ws]
        + [pl.BlockSpec((None, ch, FILT_HID), sel), pl.BlockSpec((None, ch, 1), sel)],
        out_specs=pl.BlockSpec((ch, tl // LANES, LANES), lambda i: (0, i, 0)),
        compiler_params=_cparams(("parallel",)), name="filt_taps",
    )(zt, *ws, lw['filt_w3_t'], lw['filt_decay_t'])


def _lanes_of(parts):
    return parts[0] if len(parts) == 1 else jnp.concatenate(parts, axis=1)


def _hyena_kernel(v_ref, x1_ref, x2_ref, kt_ref, cw_ref, cb_ref, d_ref, msig_ref, mfilt_ref, minv_ref,
                  g2f_ref, g2i_ref, twr_ref, twi_ref, o_ref, kf_sc, *, scale):
    _, cb, nh, _ = v_ref.shape
    n1 = 2 * nh
    twr = twr_ref[...]
    twi = twi_ref[...]

    def fwd(s_all, m):
        a = _dot(m, s_all)
        rows = []
        for c in range(cb):
            ar = a[:n1, c * LANES:(c + 1) * LANES]
            ai = a[n1:, c * LANES:(c + 1) * LANES]
            rows.append(jnp.concatenate([ar * twr - ai * twi, ar * twi + ai * twr], axis=1).astype(BF))
        return _dot(jnp.concatenate(rows, axis=0), g2f_ref[...])

    def inv(z):
        b = _dot(z.astype(BF), g2i_ref[...])
        cols = []
        for c in range(cb):
            br = b[c * n1:(c + 1) * n1, :LANES]
            bi = b[c * n1:(c + 1) * n1, LANES:]
            cols.append(jnp.concatenate([br * twr + bi * twi, bi * twr - br * twi], axis=0).astype(BF))
        return _dot(minv_ref[...], _lanes_of(cols))

    def cmul(x, kf):
        xr, xi = x[:, :LANES], x[:, LANES:]
        kr, ki = kf[:, :LANES], kf[:, LANES:]
        return jnp.concatenate([xr * kr - xi * ki, xr * ki + xi * kr], axis=1)

    @pl.when(pl.program_id(1) == 0)
    def _():
        for o in range(2):
            taps = _lanes_of([kt_ref[o, c] for c in range(cb)]).astype(BF)
            kf_sc[o] = fwd(taps, mfilt_ref[...]) * scale

    row = lax.broadcasted_iota(jnp.int32, (cb * nh, LANES), 0) % nh
    lane = lax.broadcasted_iota(jnp.int32, (cb * nh, LANES), 1)

    def short_conv(ref, g):
        w = [jnp.broadcast_to(cw_ref[k, g], (cb, nh, LANES)).reshape(cb * nh, LANES) for k in range(3)]
        bias = jnp.broadcast_to(cb_ref[g], (cb, nh, LANES)).reshape(cb * nh, LANES)
        halves = []
        for r in range(2):
            u = ref[r].astype(F32).reshape(cb * nh, LANES)
            lp = pltpu.roll(u, 1, 1)
            prev = jnp.where(lane == 0, pltpu.roll(lp, 1, 0), lp)
            prev = jnp.where((lane == 0) & (row == 0), 0.0, prev)
            ln = pltpu.roll(u, LANES - 1, 1)
            nxt = jnp.where(lane == LANES - 1, pltpu.roll(ln, cb * nh - 1, 0), ln)
            nxt = jnp.where((lane == LANES - 1) & (row == nh - 1), 0.0, nxt)
            y = prev * w[0] + u * w[1] + nxt * w[2] + bias
            halves.append(_lanes_of([y[c * nh:(c + 1) * nh] for c in range(cb)]))
        return jnp.concatenate(halves, axis=0)

    def skip(o):
        return _lanes_of([d_ref[o, c] for c in range(cb)])

    v = short_conv(v_ref, 0)
    x1 = short_conv(x1_ref, 1)
    x2 = short_conv(x2_ref, 2)
    y = inv(cmul(fwd(v.astype(BF), msig_ref[...]), kf_sc[0]))
    z1 = (x1 * (y + skip(0) * v)).astype(BF)
    y = inv(cmul(fwd(z1, msig_ref[...]), kf_sc[1]))
    hy = x2 * (y + skip(1) * z1.astype(F32))
    for c in range(cb):
        o_ref[0, :, c, :] = hy[:nh, c * LANES:(c + 1) * LANES].astype(o_ref.dtype)
        o_ref[1, :, c, :] = hy[nh:, c * LANES:(c + 1) * LANES].astype(o_ref.dtype)


def _hyena(hv, hx1, hx2, taps, lw, plan, B, L):
    nh = L // LANES
    n1 = 2 * nh
    P = B // 2
    cb = HY_CB
    pair = lambda a: a.reshape(P, 2, HY_W, nh, LANES)
    sig = pl.BlockSpec((None, 2, cb, nh, LANES), lambda j, p: (p, 0, j, 0, 0))
    consts = [plan['m_sig'], plan['m_filt'], plan['m_inv'], plan['g2f'], plan['g2i'], plan['twr'], plan['twi']]
    out = pl.pallas_call(
        functools.partial(_hyena_kernel, scale=1.0 / (2 * L)),
        out_shape=jax.ShapeDtypeStruct((P, 2, nh, HY_W, LANES), HY_DT),
        grid=(HY_W // cb, P),
        in_specs=[sig, sig, sig,
                  pl.BlockSpec((2, cb, n1, LANES), lambda j, p: (0, j, 0, 0)),
                  pl.BlockSpec((3, 3, cb, 1, LANES), lambda j, p: (0, 0, j, 0, 0)),
                  pl.BlockSpec((3, cb, 1, LANES), lambda j, p: (0, j, 0, 0)),
                  pl.BlockSpec((2, cb, 1, LANES), lambda j, p: (0, j, 0, 0))] + [_full(c) for c in consts],
        out_specs=pl.BlockSpec((None, 2, nh, cb, LANES), lambda j, p: (p, 0, 0, j, 0)),
        scratch_shapes=[pltpu.VMEM((2, cb * n1, 2 * LANES), F32)],
        compiler_params=_cparams(("parallel", "arbitrary")), name="hyena",
    )(pair(hv), pair(hx1), pair(hx2), taps.reshape(2, HY_W, n1, LANES), lw['conv_w_t'], lw['conv_b_t'],
      lw['hyena_d_t'], *consts)
    return out.reshape(B, nh, HY_W, LANES)


def _fft_plan(L):
    n = 2 * L
    n2 = LANES
    n1 = n // n2
    i1 = jnp.arange(n1, dtype=jnp.int32)
    i2 = jnp.arange(n2, dtype=jnp.int32)
    a1 = (2.0 * math.pi / n1) * ((i1[:, None] * i1[None, :]) % n1).astype(F32)
    c1, s1 = jnp.cos(a1), jnp.sin(a1)
    hn = n1 // 2
    m_sig = jnp.concatenate([jnp.concatenate([c1[:, :hn], s1[:, :hn]], 1),
                             jnp.concatenate([-s1[:, :hn], c1[:, :hn]], 1)], 0).astype(BF)
    m_filt = jnp.concatenate([c1, -s1], 0).astype(BF)
    m_inv = jnp.concatenate([jnp.concatenate([c1[:hn], -s1[:hn]], 1),
                             jnp.concatenate([s1[:hn], c1[:hn]], 1)], 0).astype(BF)
    a2 = (2.0 * math.pi / n2) * ((i2[:, None] * i2[None, :]) % n2).astype(F32)
    c2, s2 = jnp.cos(a2), jnp.sin(a2)
    g2f = jnp.concatenate([jnp.concatenate([c2, -s2], 1), jnp.concatenate([s2, c2], 1)], 0).astype(BF)
    g2i = jnp.concatenate([jnp.concatenate([c2, s2], 1), jnp.concatenate([-s2, c2], 1)], 0).astype(BF)
    at = (2.0 * math.pi / n) * (i1[:, None] * i2[None, :]).astype(F32)
    return dict(m_sig=m_sig, m_filt=m_filt, m_inv=m_inv, g2f=g2f, g2i=g2i, twr=jnp.cos(at), twi=-jnp.sin(at))


def _pick(total, want):
    b = min(total, want)
    while total % b:
        b //= 2
    return b


def _out_proj_kernel(x_ref, a_ref, hy_ref, ga_ref, gh_ref, wa_ref, wh_ref, o_ref):
    a = a_ref[...].astype(F32)
    ra = lax.rsqrt(jnp.sum(a * a, axis=-1, keepdims=True) * (1.0 / ATTN_W) + EPS)
    an = (a * ra * ga_ref[...]).astype(BF)
    acc = x_ref[...] + _dot(an, wa_ref[...])
    gh = gh_ref[...]
    wh = wh_ref[...]
    for s in range(hy_ref.shape[0]):
        hy = hy_ref[s].astype(F32)
        rh = lax.rsqrt(jnp.sum(hy * hy, axis=0, keepdims=True) * (1.0 / HY_W) + EPS)
        hn = (hy * rh * gh).astype(BF)
        part = lax.dot_general(hn, wh, (((0,), (0,)), ((), ())), preferred_element_type=F32)
        o_ref[s * LANES:(s + 1) * LANES, :] = acc[s * LANES:(s + 1) * LANES, :] + part


def _out_proj(x2d, attn, hy, lw, B, L, tm):
    T = x2d.shape[0]
    tps = L // tm
    ws = [lw['ga'], lw['gh_t'], lw['w_out_a'], lw['w_out_h']]
    return pl.pallas_call(
        _out_proj_kernel,
        out_shape=jax.ShapeDtypeStruct((T, D_MODEL), F32),
        grid=(T // tm,),
        in_specs=[pl.BlockSpec((tm, D_MODEL), lambda i: (i, 0)),
                  pl.BlockSpec((tm, PAD_W), lambda i: (i, 0)),
                  pl.BlockSpec((None, tm // LANES, HY_W, LANES), lambda i: (i // tps, i % tps, 0, 0))]
        + [_full(w) for w in ws],
        out_specs=pl.BlockSpec((tm, D_MODEL), lambda i: (i, 0)),
        compiler_params=_cparams(("parallel",)), name="out_proj",
    )(x2d, attn, hy, *ws)


def _silu_mul(g, u):
    return g * (1.0 / (1.0 + jnp.exp(-g))) * u


def _ffn_kernel(x_ref, g_ref, wg_ref, wu_ref, wd_ref, o_ref, hn_sc, acc_sc):
    j = pl.program_id(1)

    @pl.when(j == 0)
    def _():
        hn_sc[...] = _rms(x_ref[...], g_ref[...]).astype(BF)
        acc_sc[...] = jnp.zeros(acc_sc.shape, F32)

    hn = hn_sc[...]
    a = _silu_mul(_dot(hn, wg_ref[...]), _dot(hn, wu_ref[...])).astype(BF)
    acc_sc[...] += _dot(a, wd_ref[...])

    @pl.when(j == pl.num_programs(1) - 1)
    def _():
        o_ref[...] = x_ref[...] + acc_sc[...]


def _ffn(x2d, g, wg, wu, wd, tm, tf):
    T = x2d.shape[0]
    ff = wg.shape[1]
    return pl.pallas_call(
        _ffn_kernel,
        out_shape=jax.ShapeDtypeStruct((T, D_MODEL), F32),
        grid=(T // tm, ff // tf),
        in_specs=[pl.BlockSpec((tm, D_MODEL), lambda i, j: (i, 0)),
                  pl.BlockSpec((1, D_MODEL), lambda i, j: (0, 0)),
                  pl.BlockSpec((D_MODEL, tf), lambda i, j: (0, j)),
                  pl.BlockSpec((D_MODEL, tf), lambda i, j: (0, j)),
                  pl.BlockSpec((tf, D_MODEL), lambda i, j: (j, 0))],
        out_specs=pl.BlockSpec((tm, D_MODEL), lambda i, j: (i, 0)),
        scratch_shapes=[pltpu.VMEM((tm, D_MODEL), BF), pltpu.VMEM((tm, D_MODEL), F32)],
        compiler_params=_cparams(("parallel", "arbitrary")), name="ffn",
    )(x2d, g, wg, wu, wd)


def _moe_kernel(x_ref, g_ref, rwt_ref, wg_ref, wu_ref, wd_ref, o_ref, hn_sc, sel_sc, pos_sc, gate_sc,
                oh_sc, g_sc, xg_sc, y_sc, *, cap):
    e = pl.program_id(1)
    j = pl.program_id(2)
    nsub, tm = sel_sc.shape[0], sel_sc.shape[2]
    last_j = pl.num_programs(2) - 1
    slot = lax.broadcasted_iota(jnp.int32, (cap, tm), 0).astype(F32)

    def expert(xg):
        h = _silu_mul(_dot(xg, wg_ref[...]), _dot(xg, wu_ref[...])).astype(BF)
        return _dot(h, wd_ref[...])

    subtiles = []
    for s in range(nsub):
        rows = slice(s * tm, (s + 1) * tm)

        @pl.when((e == 0) & (j == 0))
        def _():
            x = x_ref[rows, :]
            hn = _rms(x, g_ref[...])
            hn_sc[s] = hn.astype(BF)
            o_ref[rows, :] = x
            lg = lax.dot_general(rwt_ref[...], hn, (((1,), (1,)), ((), ())), precision=HI,
                                 preferred_element_type=F32)
            row = lax.broadcasted_iota(jnp.int32, lg.shape, 0)
            m1 = jnp.max(lg, axis=0, keepdims=True)
            i1 = jnp.min(jnp.where(lg == m1, row, N_EXPERTS), axis=0, keepdims=True)
            lg2 = jnp.where(row == i1, -jnp.inf, lg)
            m2 = jnp.max(lg2, axis=0, keepdims=True)
            i2 = jnp.min(jnp.where(lg2 == m2, row, N_EXPERTS), axis=0, keepdims=True)
            e2 = jnp.exp(m2 - m1)
            g1 = 1.0 / (1.0 + e2)
            gate_sc[s] = jnp.where(row == i1, g1, 0.0) + jnp.where(row == i2, e2 * g1, 0.0)
            sel = jnp.where((row == i1) | (row == i2), 1.0, 0.0)
            sel_sc[s] = sel
            r = lax.broadcasted_iota(jnp.int32, (tm, tm), 0)
            c = lax.broadcasted_iota(jnp.int32, (tm, tm), 1)
            pos_sc[s] = _dot(sel.astype(BF), jnp.where(r < c, 1.0, 0.0).astype(BF))

        sel_row = sel_sc[s, pl.ds(e, 1), :]
        pos_row = pos_sc[s, pl.ds(e, 1), :]
        gate_row = gate_sc[s, pl.ds(e, 1), :]
        count = jnp.sum(sel_row).astype(jnp.int32)

        def one_hot(c, sel_row=sel_row, pos_row=pos_row, gate_row=gate_row):
            base = jnp.asarray(c * cap).astype(F32)
            onehot = jnp.where((pos_row - base == slot) & (sel_row > 0.0), 1.0, 0.0)
            gate = jnp.sum(onehot * gate_row, axis=1, keepdims=True)
            return onehot.astype(BF), gate

        def scatter(onehot, y, rows=rows):
            o_ref[rows, :] += lax.dot_general(onehot, y.astype(BF), (((0,), (0,)), ((), ())),
                                              preferred_element_type=F32)

        @pl.when(j == 0)
        def _():
            onehot, gate = one_hot(0)
            oh_sc[s] = onehot
            g_sc[s] = gate
            xg_sc[s * cap:(s + 1) * cap, :] = _dot(onehot, hn_sc[s]).astype(BF)

        subtiles.append((s, count, one_hot, scatter))

    @pl.when(j == 0)
    def _():
        y_sc[...] = jnp.zeros(y_sc.shape, F32)

    y_sc[...] += expert(xg_sc[...])

    for s, count, one_hot, scatter in subtiles:
        @pl.when(j == last_j)
        def _():
            scatter(oh_sc[s], y_sc[s * cap:(s + 1) * cap, :] * g_sc[s])

        def chunk(c, carry, s=s, one_hot=one_hot, scatter=scatter):
            onehot, gate = one_hot(c)
            xg = _dot(onehot, hn_sc[s]).astype(BF)
            scatter(onehot, expert(xg) * gate)
            return carry

        lax.fori_loop(1, (count + cap - 1) // cap, chunk, 0)


def _moe(x2d, g, rwt, wg, wu, wd, tm, nsub, tf, cap):
    T = x2d.shape[0]
    ne, _, ff = wg.shape
    tb = tm * nsub
    return pl.pallas_call(
        functools.partial(_moe_kernel, cap=cap),
        out_shape=jax.ShapeDtypeStruct((T, D_MODEL), F32),
        grid=(T // tb, ne, ff // tf),
        in_specs=[pl.BlockSpec((tb, D_MODEL), lambda i, e, j: (i, 0), pipeline_mode=pl.Buffered(1)),
                  pl.BlockSpec((1, D_MODEL), lambda i, e, j: (0, 0)),
                  pl.BlockSpec((ne, D_MODEL), lambda i, e, j: (0, 0)),
                  pl.BlockSpec((None, D_MODEL, tf), lambda i, e, j: (e, 0, j)),
                  pl.BlockSpec((None, D_MODEL, tf), lambda i, e, j: (e, 0, j)),
                  pl.BlockSpec((None, tf, D_MODEL), lambda i, e, j: (e, j, 0))],
        out_specs=pl.BlockSpec((tb, D_MODEL), lambda i, e, j: (i, 0)),
        scratch_shapes=[pltpu.VMEM((nsub, tm, D_MODEL), BF), pltpu.VMEM((nsub, ne, tm), F32),
                        pltpu.VMEM((nsub, ne, tm), F32), pltpu.VMEM((nsub, ne, tm), F32),
                        pltpu.VMEM((nsub, cap, tm), BF), pltpu.VMEM((nsub, cap, 1), F32),
                        pltpu.VMEM((nsub * cap, D_MODEL), BF), pltpu.VMEM((nsub * cap, D_MODEL), F32)],
        compiler_params=_cparams(("parallel", "arbitrary", "arbitrary"), MOE_VMEM_LIMIT), name="moe",
    )(x2d, g, rwt, wg, wu, wd)


def _pad_heads(w, d):
    lead = w.shape[:-1]
    w = w.reshape(lead + (N_HEADS, d))
    w = jnp.pad(w, [(0, 0)] * len(lead) + [(0, 0), (0, HEAD_PAD - d)])
    return w.reshape(lead + (PAD_W,))


def _per_channel(a):
    return jnp.broadcast_to(a.astype(F32)[..., None, None], a.shape + (1, LANES))


def _layer_weights(p, layer):
    (mix_norm, w_in, cq_norm, ckv_norm, w_uq, w_ukv, q_norm, k_norm, conv_w, conv_b,
     filt_w1, filt_b1, filt_freq, filt_w2, filt_b2, filt_w3, filt_decay, hyena_d,
     out_norm, w_out, ffn_norm) = [a[layer] for a in p]
    na = Q_LORA + KV_LORA + QK_ROPE
    ukv = w_ukv.reshape(KV_LORA, N_HEADS, QK_NOPE + V_DIM)
    eye = jnp.eye(QK_ROPE, dtype=F32)
    p_kr = jnp.pad(eye, ((0, 0), (QK_NOPE, HEAD_PAD - QK_DIM)))
    row = lambda a: a.reshape(1, -1).astype(F32)
    col = lambda a: a.reshape(-1, 1).astype(F32)
    half = QK_ROPE // 2
    uq = w_uq.reshape(Q_LORA, N_HEADS, QK_DIM)
    swap = lambda a: jnp.concatenate([a[..., half:], a[..., :half]], -1)
    swap_gain = lambda g: jnp.concatenate([g[:QK_NOPE], swap(g[QK_NOPE:])])
    gain_row = lambda g: jnp.pad(g, (0, HEAD_PAD - QK_DIM)).reshape(1, HEAD_PAD).astype(F32)
    q_scale = QK_DIM ** -0.5 * math.log2(math.e)
    w_out_a = w_out[:ATTN_W].reshape(N_HEADS, V_DIM, D_MODEL)
    w_out_a = jnp.pad(w_out_a, ((0, 0), (0, HEAD_PAD - V_DIM), (0, 0))).reshape(PAD_W, D_MODEL)
    w3 = filt_w3.reshape(FILT_HID, 2, 2, HY_W).transpose(2, 1, 3, 0).reshape(2, 2 * HY_W, FILT_HID)
    dec = filt_decay.reshape(2, 2, HY_W).transpose(1, 0, 2).reshape(2, 2 * HY_W, 1)
    return dict(
        mix_norm=row(mix_norm), w_a=w_in[:, :na].astype(BF), w_hy_t=w_in[:, na:].T.astype(BF),
        cq_norm=row(cq_norm), ckv_norm=row(ckv_norm),
        w_uq=_pad_heads(w_uq, QK_DIM).astype(BF),
        w_uk=_pad_heads(ukv[:, :, :QK_NOPE].reshape(KV_LORA, -1), QK_NOPE).astype(BF),
        w_uv=_pad_heads(ukv[:, :, QK_NOPE:].reshape(KV_LORA, -1), V_DIM).astype(BF),
        w_uq_sw=_pad_heads(jnp.concatenate([jnp.zeros_like(uq[:, :, :QK_NOPE]), swap(uq[:, :, QK_NOPE:])], -1)
                           .reshape(Q_LORA, -1), QK_DIM).astype(BF),
        p_kr=jnp.tile(p_kr, (1, N_HEADS)).astype(BF),
        p_kr_sw=jnp.tile(jnp.pad(swap(eye), ((0, 0), (QK_NOPE, HEAD_PAD - QK_DIM))), (1, N_HEADS)).astype(BF),
        gq_cos=gain_row(q_norm) * q_scale, gq_sin=gain_row(swap_gain(q_norm)) * q_scale,
        gk_cos=gain_row(k_norm), gk_sin=gain_row(swap_gain(k_norm)),
        conv_w_t=_per_channel(conv_w.reshape(3, 3, HY_W)), conv_b_t=_per_channel(conv_b.reshape(3, HY_W)),
        filt_w1_t=jnp.pad(filt_w1, ((0, LANES - POS_EMB), (0, 0))).T, filt_b1=col(filt_b1),
        filt_freq=col(filt_freq), filt_w2_t=filt_w2.T, filt_b2=col(filt_b2), filt_w3_t=w3.astype(BF),
        filt_decay_t=dec,
        hyena_d_t=_per_channel(hyena_d),
        ga=_pad_heads(out_norm[:ATTN_W], V_DIM).reshape(1, PAD_W),
        gh_t=jnp.broadcast_to(col(out_norm[ATTN_W:]), (HY_W, LANES)),
        w_out_a=w_out_a.astype(BF), w_out_h=w_out[ATTN_W:].astype(BF),
        ffn_norm=row(ffn_norm),
    )


def _position_tables(L):
    pos = jnp.arange(L, dtype=F32)
    half = QK_ROPE // 2
    inv = ROPE_THETA ** (-jnp.arange(0, QK_ROPE, 2, dtype=F32) / QK_ROPE)
    ang = pos[:, None] * inv[None, :]
    cos, sin = jnp.cos(ang), jnp.sin(ang)
    zeros = lambda w: jnp.zeros((L, w), F32)
    cos_t = jnp.concatenate([jnp.ones((L, QK_NOPE), F32), cos, cos, zeros(HEAD_PAD - QK_DIM)], 1)
    sin_t = jnp.concatenate([zeros(QK_NOPE), -sin, sin, zeros(HEAD_PAD - QK_DIM)], 1)
    n = jnp.arange(2 * L, dtype=jnp.int32)
    p = jnp.where(n < L, n, 2 * L - n)
    p = jnp.where(n == L, 0, p)
    t = (p.astype(F32) / (L - 1))[None, :]
    w = 2.0 * math.pi * p.astype(F32) / L
    f = jnp.linspace(1e-4, N_BANDS - 1, N_BANDS, dtype=F32)
    a = f[:, None] * w[None, :]
    zt = jnp.concatenate([t, jnp.cos(a), -jnp.sin(a), jnp.zeros((LANES - POS_EMB, 2 * L), F32)], axis=0)
    return (cos_t, sin_t), zt


def _moe_capacity(tm):
    return max(16, (tm * 9 // 32) // 16 * 16)


def _trunk(x, layers, mixers, plan, rope, zt):
    B, L, _ = x.shape
    T = B * L
    x2d = x.reshape(T, D_MODEL)
    tm = _pick(L, 512)
    for lw, mixer in zip(layers, mixers):
        q, k, v = _in_proj(x2d, L, lw, rope, tm)
        hv, hx1, hx2 = _hy_proj(x2d, B, L, lw, _pick(L, 2048))
        attn = _attn(q, k, v, B, L, _pick(L, ATTN_TQ))
        taps = _filter_taps(zt, lw, L, _pick(L, 1024))
        hy = _hyena(hv, hx1, hx2, taps, lw, plan, B, L)
        x2d = _out_proj(x2d, attn, hy, lw, B, L, tm)
        tf = 1408
        if mixer[0] == 'dense':
            _, wg, wu, wd = mixer
            x2d = _ffn(x2d, lw['ffn_norm'], wg, wu, wd, _pick(T, 1024), tf)
        else:
            _, rwt, wg, wu, wd = mixer
            tme = _pick(T, MOE_TM)
            x2d = _moe(x2d, lw['ffn_norm'], rwt, wg, wu, wd, tme, _pick(T // tme, MOE_SUBTILES), tf,
                       _moe_capacity(tme))
    return x2d.reshape(B, L, D_MODEL)


def kernel(x_prompt, x_sample, mix_norm, w_in, cq_norm, ckv_norm, w_uq, w_ukv, q_norm, k_norm, conv_w, conv_b, filt_w1, filt_b1, filt_freq, filt_w2, filt_b2, filt_w3, filt_decay, hyena_d, out_norm, w_out, ffn_norm, dense_wg, dense_wu, dense_wd, router_w, moe_wg, moe_wu, moe_wd):
    per_layer = (mix_norm, w_in, cq_norm, ckv_norm, w_uq, w_ukv, q_norm, k_norm, conv_w, conv_b,
                 filt_w1, filt_b1, filt_freq, filt_w2, filt_b2, filt_w3, filt_decay, hyena_d,
                 out_norm, w_out, ffn_norm)
    depth = mix_norm.shape[0]
    layers = [_layer_weights(per_layer, l) for l in range(depth)]
    mixers = []
    for l in range(depth):
        j = l // 2
        if l % 2 == 0:
            mixers.append(('dense', dense_wg[j].astype(BF), dense_wu[j].astype(BF), dense_wd[j].astype(BF)))
        else:
            mixers.append(('moe', router_w[j].T, moe_wg[j].astype(BF), moe_wu[j].astype(BF), moe_wd[j].astype(BF)))
    outs = []
    for x in (x_prompt, x_sample):
        L = x.shape[1]
        rope, zt = _position_tables(L)
        outs.append(_trunk(x, layers, mixers, _fft_plan(L), rope, zt))
    return tuple(outs)
```
